```python
import math
import jax, jax.numpy as jnp
from jax import lax
import numpy as np

D_MODEL = 1024
BATCH = 16
SEQ = 4096
DEPTH = 1

MIX_WIDTH = D_MODEL
GLA_WIDTH = MIX_WIDTH // 2
GLA_HEADS = 4
GLA_DV = GLA_WIDTH // GLA_HEADS
GLA_DK = GLA_DV // 2
GLA_KEY_WIDTH = GLA_HEADS * GLA_DK
GATE_RANK = 16
GATE_TAU = 16.0
CHUNK = 64
FNET_WIDTH = MIX_WIDTH - GLA_WIDTH
FNET_GROUPS = 4
FNET_GDIM = FNET_WIDTH // FNET_GROUPS
LN_EPS = 1e-5
RMS_EPS = 1e-6
DEEPNORM_ALPHA = (2.0 * DEPTH) ** 0.25
DEEPNORM_BETA = (8.0 * DEPTH) ** -0.25

_SPLITS = [GLA_KEY_WIDTH,
           GLA_KEY_WIDTH,
           GLA_WIDTH,
           GATE_RANK,
           GATE_RANK,
           GLA_WIDTH,
           FNET_WIDTH,
           FNET_WIDTH]
IN_WIDTH = sum(_SPLITS)
_OFFSETS = list(np.cumsum(_SPLITS)[:-1])

kernel_name = "hybrid_gla_fnet_deepnorm_encoder"


def _gla_chunked(q, k, v, log_a):
    B, S, H, DK = q.shape
    DV = v.shape[-1]
    n = S // CHUNK

    def to_chunks(t):
        return t.reshape(B, n, CHUNK, H, t.shape[-1]).transpose(1, 0, 3, 2, 4)

    qc, kc, vc, gc = (to_chunks(t) for t in (q, k, v, log_a))
    bc = jnp.cumsum(gc, axis=3)
    mask = jnp.tril(jnp.ones((CHUNK, CHUNK), dtype=bool))[:, :, None]

    def step(state, inp):
        qi, ki, vi, bi = inp
        b_last = bi[:, :, -1:, :]
        diff = bi[:, :, :, None, :] - bi[:, :, None, :, :]
        decay = jnp.exp(jnp.where(mask, diff, -jnp.inf))
        scores = jnp.einsum('bhid,bhjd,bhijd->bhij', qi, ki, decay)
        o = (jnp.einsum('bhij,bhjv->bhiv', scores, vi)
             + jnp.einsum('bhid,bhdv->bhiv', qi * jnp.exp(bi), state))
        k_dec = ki * jnp.exp(b_last - bi)
        state = (jnp.exp(b_last)[:, :, 0, :, None] * state
                 + jnp.einsum('bhjd,bhjv->bhdv', k_dec, vi))
        return state, o

    state0 = jnp.zeros((B, H, DK, DV), jnp.float32)
    _, o = lax.scan(step, state0, (qc, kc, vc, bc))
    return o.transpose(1, 0, 3, 2, 4).reshape(B, S, H, DV)


def _layernorm(x, g, b):
    xf = x.astype(jnp.float32)
    mu = jnp.mean(xf, axis=-1, keepdims=True)
    var = jnp.mean(jnp.square(xf - mu), axis=-1, keepdims=True)
    y = (xf - mu) * lax.rsqrt(var + LN_EPS) * g.astype(jnp.float32) + b.astype(jnp.float32)
    return y.astype(x.dtype)


def setup_inputs(seed: int = 0) -> dict:
    key = jax.random.key(seed)
    ks = jax.random.split(key, 12)
    x = jax.random.normal(ks[0], (BATCH, SEQ, D_MODEL), jnp.float32)
    w_in = jax.random.normal(ks[1], (DEPTH, D_MODEL, IN_WIDTH), jnp.float32) * D_MODEL ** -0.5
    v0, v1 = 2 * GLA_KEY_WIDTH, 2 * GLA_KEY_WIDTH + GLA_WIDTH
    w_in = w_in.at[:, :, v0:v1].multiply(DEEPNORM_BETA)
    w_gate_up_fwd = jax.random.normal(ks[2], (DEPTH, GATE_RANK, GLA_KEY_WIDTH), jnp.float32) * GATE_RANK ** -0.5
    b_gate_fwd = jax.random.normal(ks[3], (DEPTH, GLA_KEY_WIDTH), jnp.float32) * 0.1
    w_gate_up_bwd = jax.random.normal(ks[4], (DEPTH, GATE_RANK, GLA_KEY_WIDTH), jnp.float32) * GATE_RANK ** -0.5
    b_gate_bwd = jax.random.normal(ks[5], (DEPTH, GLA_KEY_WIDTH), jnp.float32) * 0.1
    gla_norm_g = 1.0 + 0.02 * jax.random.normal(ks[6], (DEPTH, GLA_WIDTH), jnp.float32)
    w_fnet = jax.random.normal(ks[7], (DEPTH, FNET_GROUPS, FNET_GDIM, FNET_GDIM), jnp.float32) * FNET_GDIM ** -0.5
    w_out = jax.random.normal(ks[8], (DEPTH, MIX_WIDTH, D_MODEL), jnp.float32) * (MIX_WIDTH ** -0.5 * DEEPNORM_BETA)
    ln_g = 1.0 + 0.02 * jax.random.normal(ks[9], (DEPTH, D_MODEL), jnp.float32)
    ln_b = 0.02 * jax.random.normal(ks[10], (DEPTH, D_MODEL), jnp.float32)
    return {"x": x, "w_in": w_in, "w_gate_up_fwd": w_gate_up_fwd, "b_gate_fwd": b_gate_fwd,
            "w_gate_up_bwd": w_gate_up_bwd, "b_gate_bwd": b_gate_bwd, "gla_norm_g": gla_norm_g,
            "w_fnet": w_fnet, "w_out": w_out, "ln_g": ln_g, "ln_b": ln_b}


def reference(x, w_in, w_gate_up_fwd, b_gate_fwd, w_gate_up_bwd, b_gate_bwd, gla_norm_g,
              w_fnet, w_out, ln_g, ln_b):
    B, S, _ = x.shape
    f32 = jnp.float32
    for l in range(DEPTH):
        proj = jnp.einsum('bsd,de->bse', x, w_in[l])
        q, k, v, gf_lr, gb_lr, z_gla, u, z_fnet = jnp.split(proj, _OFFSETS, axis=-1)

        qh = (q.astype(f32) * (GLA_DK ** -0.5)).reshape(B, S, GLA_HEADS, GLA_DK)
        kh = k.astype(f32).reshape(B, S, GLA_HEADS, GLA_DK)
        vh = v.astype(f32).reshape(B, S, GLA_HEADS, GLA_DV)
        log_af = jax.nn.log_sigmoid(
            jnp.einsum('bsr,rk->bsk', gf_lr.astype(f32), w_gate_up_fwd[l].astype(f32))
            + b_gate_fwd[l].astype(f32)) / GATE_TAU
        log_ab = jax.nn.log_sigmoid(
            jnp.einsum('bsr,rk->bsk', gb_lr.astype(f32), w_gate_up_bwd[l].astype(f32))
            + b_gate_bwd[l].astype(f32)) / GATE_TAU
        log_af = log_af.reshape(B, S, GLA_HEADS, GLA_DK)
        log_ab = log_ab.reshape(B, S, GLA_HEADS, GLA_DK)
        o_fwd = _gla_chunked(qh, kh, vh, log_af)
        flip = lambda t: jnp.flip(t, axis=1)
        o_bwd = flip(_gla_chunked(flip(qh), flip(kh), flip(vh), flip(log_ab)))
        o = o_fwd + o_bwd
        o = o * lax.rsqrt(jnp.mean(jnp.square(o), axis=-1, keepdims=True) + RMS_EPS)
        o = o.reshape(B, S, GLA_WIDTH) * gla_norm_g[l].astype(f32)
        y_gla = o * jax.nn.silu(z_gla.astype(f32))

        ug = u.astype(f32).reshape(B, S, FNET_GROUPS, FNET_GDIM)
        uf = jnp.fft.fft2(ug, axes=(1, 3), norm='ortho').real.astype(f32)
        uf = jnp.einsum('bsgc,gcd->bsgd', uf, w_fnet[l].astype(f32)).reshape(B, S, FNET_WIDTH)
        y_fnet = uf * jax.nn.silu(z_fnet.astype(f32))

        mix = jnp.concatenate([y_gla, y_fnet], axis=-1).astype(x.dtype)
        y = jnp.einsum('bse,ed->bsd', mix, w_out[l])

        x = _layernorm(DEEPNORM_ALPHA * x + y, ln_g[l], ln_b[l])
    return x
```

```python
import functools
import math

import numpy as np
import jax
import jax.numpy as jnp
from jax import lax
from jax.experimental import pallas as pl
from jax.experimental.pallas import tpu as pltpu

F32 = jnp.float32
BF16 = jnp.bfloat16

D_MODEL = 1024
SEQ = 4096
GLA_HEADS = 4
GLA_DK = 64
GLA_DV = 128
GLA_KEY_WIDTH = GLA_HEADS * GLA_DK
GLA_WIDTH = GLA_HEADS * GLA_DV
GATE_RANK = 16
GATE_TAU = 16.0
FNET_GROUPS = 4
FNET_GDIM = 128
FNET_WIDTH = FNET_GROUPS * FNET_GDIM
LN_EPS = 1e-5
RMS_EPS = 1e-6
DEPTH = 1
DEEPNORM_ALPHA = (2.0 * DEPTH) ** 0.25

LANES = 128
SUBLANES = 8
VMEM_LIMIT_BYTES = 56 * 1024 * 1024

ROW_TILE = 512
GLA_CHUNK = 256
GATE_PAD = LANES
RADIX = 16
FNET_HALF = 256

_PROJ_COLS = (("q", GLA_KEY_WIDTH), ("k", GLA_KEY_WIDTH), ("v", GLA_WIDTH), ("zg", GLA_WIDTH),
              ("u", FNET_WIDTH), ("zf", FNET_WIDTH), ("g", GATE_PAD))
PROJ_WIDTH = sum(w for _, w in _PROJ_COLS)


def _dot(a, b):
    return jnp.dot(a, b, preferred_element_type=F32)


def _split2(x):
    hi = x.astype(BF16)
    lo = (x - hi.astype(F32)).astype(BF16)
    return hi, lo


def _split3(x):
    hi = x.astype(BF16)
    r = x - hi.astype(F32)
    mid = r.astype(BF16)
    lo = (r - mid.astype(F32)).astype(BF16)
    return hi, mid, lo


def _dot_f32(a, b):
    ah, al = _split2(a)
    bh, bl = _split2(b)
    return _dot(ah, bh) + _dot(ah, bl) + _dot(al, bh)


def _silu(z):
    return z / (1.0 + jnp.exp(-z))


def _proj_kernel(x_ref, w_ref, q_ref, k_ref, v_ref, zg_ref, u_ref, zf_ref, g_ref):
    xb = x_ref[...].astype(BF16)
    outs = (q_ref, k_ref, v_ref, zg_ref, u_ref, zf_ref, g_ref)
    off = 0
    for (_, width), o_ref in zip(_PROJ_COLS, outs):
        o_ref[...] = _dot(xb, w_ref[:, off:off + width]).astype(o_ref.dtype)
        off += width


def _projection(x2d, w_perm):
    m = x2d.shape[0]
    grid = (m // ROW_TILE,)
    row = lambda i: (i, 0)
    out_shape = [jax.ShapeDtypeStruct((m, w), F32 if name == "g" else BF16) for name, w in _PROJ_COLS]
    out_specs = [pl.BlockSpec((ROW_TILE, w), row) for _, w in _PROJ_COLS]
    return pl.pallas_call(
        _proj_kernel,
        grid=grid,
        in_specs=[pl.BlockSpec((ROW_TILE, D_MODEL), row),
                  pl.BlockSpec((D_MODEL, PROJ_WIDTH), lambda i: (0, 0))],
        out_specs=out_specs,
        out_shape=out_shape,
        compiler_params=pltpu.CompilerParams(dimension_semantics=("arbitrary",),
                                             vmem_limit_bytes=VMEM_LIMIT_BYTES),
        name="in_proj",
    )(x2d, w_perm)


def _gla_kernel(q_ref, k_ref, v_ref, g_ref, zg_ref, wg_ref, bg_ref, gam_ref, y_ref, oacc_ref):
    C = GLA_CHUNK
    n_chunks = SEQ // C
    row_i = lax.broadcasted_iota(jnp.int32, (C, C), 0)
    col_i = lax.broadcasted_iota(jnp.int32, (C, C), 1)
    lower = row_i >= col_i
    upper = row_i <= col_i
    lane_head = lax.broadcasted_iota(jnp.int32, (C, GLA_KEY_WIDTH), 1) // GLA_DK

    def chunk(n, state, reverse):
        r0 = pl.multiple_of(n * C, C)
        rows = pl.ds(r0, C)
        gcols = slice(GLA_KEY_WIDTH, 2 * GLA_KEY_WIDTH) if reverse else slice(0, GLA_KEY_WIDTH)
        z = _dot_f32(g_ref[rows, :], wg_ref[:, gcols]) + bg_ref[:, gcols]
        log_a = (jnp.minimum(z, 0.0) - jnp.log1p(jnp.exp(-jnp.abs(z)))) * (1.0 / GATE_TAU)
        tri = (upper if reverse else lower).astype(BF16)
        hi, mid, lo = _split3(log_a)
        b = _dot(tri, hi) + _dot(tri, mid) + _dot(tri, lo)
        b_edge = b[0:1, :] if reverse else b[C - 1:C, :]

        qf = q_ref[rows, :].astype(F32) * (GLA_DK ** -0.5)
        kf = k_ref[rows, :].astype(F32)
        q_dec = (qf * jnp.exp(b)).astype(BF16)
        k_inv = (kf * jnp.exp(-b)).astype(BF16)
        k_dec = kf * jnp.exp(b_edge - b)
        k_dec_t = jnp.transpose(k_dec).astype(BF16)
        vb = v_ref[rows, :]
        state_b = state.astype(BF16)
        causal = upper if reverse else lower

        new_rows = []
        for h in range(GLA_HEADS):
            vcols = slice(h * GLA_DV, (h + 1) * GLA_DV)
            q_h = jnp.where(lane_head == h, q_dec, jnp.zeros_like(q_dec))
            s = lax.dot_general(q_h, k_inv, (((1,), (1,)), ((), ())), preferred_element_type=F32)
            p = jnp.where(causal, s, 0.0).astype(BF16)
            o_h = _dot(p, vb[:, vcols]) + _dot(q_h, state_b)
            if reverse:
                o_h = o_h + oacc_ref[rows, vcols]
                ms = jnp.mean(o_h * o_h, axis=-1, keepdims=True)
                o_n = o_h * lax.rsqrt(ms + RMS_EPS) * gam_ref[:, vcols]
                zg = zg_ref[rows, vcols].astype(F32)
                y_ref[rows, vcols] = (o_n * _silu(zg)).astype(y_ref.dtype)
            else:
                oacc_ref[rows, vcols] = o_h
            krows = slice(h * GLA_DK, (h + 1) * GLA_DK)
            new_rows.append(_dot(k_dec_t[krows, :], vb[:, vcols]))
        update = jnp.concatenate(new_rows, axis=0)
        decay_t = jnp.transpose(jnp.broadcast_to(jnp.exp(b_edge), (LANES, GLA_KEY_WIDTH)))
        return state * decay_t + update

    state0 = jnp.zeros((GLA_KEY_WIDTH, GLA_DV), F32)
    lax.fori_loop(0, n_chunks, lambda n, st: chunk(n, st, False), state0)
    lax.fori_loop(0, n_chunks, lambda n, st: chunk(n_chunks - 1 - n, st, True), state0)


def _gla(q, k, v, g, zg, wg, bg, gamma, batch):
    tok = lambda b: (b, 0)
    const = lambda b: (0, 0)
    return pl.pallas_call(
        _gla_kernel,
        grid=(batch,),
        in_specs=[pl.BlockSpec((SEQ, GLA_KEY_WIDTH), tok),
                  pl.BlockSpec((SEQ, GLA_KEY_WIDTH), tok),
                  pl.BlockSpec((SEQ, GLA_WIDTH), tok),
                  pl.BlockSpec((SEQ, GATE_PAD), tok),
                  pl.BlockSpec((SEQ, GLA_WIDTH), tok),
                  pl.BlockSpec((GATE_PAD, 2 * GLA_KEY_WIDTH), const),
                  pl.BlockSpec((1, 2 * GLA_KEY_WIDTH), const),
                  pl.BlockSpec((1, GLA_WIDTH), const)],
        out_specs=pl.BlockSpec((SEQ, GLA_WIDTH), tok),
        out_shape=jax.ShapeDtypeStruct((batch * SEQ, GLA_WIDTH), BF16),
        scratch_shapes=[pltpu.VMEM((SEQ, GLA_WIDTH), F32)],
        compiler_params=pltpu.CompilerParams(dimension_semantics=("arbitrary",),
                                             vmem_limit_bytes=VMEM_LIMIT_BYTES),
        name="gla",
    )(q, k, v, g, zg, wg, bg, gamma)


def _fnet_tables():
    R = RADIX
    n = SEQ
    a = np.arange(R)
    lo8 = np.arange(SUBLANES)
    eye8 = np.eye(SUBLANES)

    ka = (8 * np.arange(2)[:, None] + lo8[None, :])
    ang = 2 * np.pi * (256 * a[None, None, :] * ka[:, :, None]) / n
    eye16 = np.eye(R)
    def s1(trig):
        t = trig(ang)
        m = t[:, None, :, :, None] * eye16[None, :, None, None, :]
        return m.reshape(2 * R * SUBLANES, R * R)
    m1 = np.concatenate([s1(np.cos), -s1(np.sin)], axis=0)

    def complex_block(theta):
        c = np.cos(theta)
        s = -np.sin(theta)
        def kron(t):
            m = t[:, :, :, None] * eye8[None, :, None, :]
            return m.reshape(t.shape[0] * SUBLANES, t.shape[2] * SUBLANES)
        mr, mi = kron(c), kron(s)
        return np.block([[mr, -mi], [mi, mr]])

    m2 = []
    for kah in range(2):
        k_a = 8 * kah + lo8
        expo = 16 * a[None, None, :] * k_a[None, :, None] + 256 * a[None, None, :] * a[:, None, None]
        m2.append(complex_block(2 * np.pi * expo / n))
    m2 = np.stack(m2)

    m3 = np.zeros((2, R, 2 * R * SUBLANES, 2 * R * SUBLANES))
    for kah in range(2):
        k_a = 8 * kah + lo8
        for kb in range(R):
            expo = (a[None, None, :] * k_a[None, :, None] + 16 * a[None, None, :] * kb
                    + 256 * a[None, None, :] * a[:, None, None])
            m3[kah, kb] = complex_block(2 * np.pi * expo / n)

    c = np.arange(FNET_GDIM)
    ang_c = 2 * np.pi * np.outer(c, c) / FNET_GDIM
    norm = 1.0 / math.sqrt(SEQ * FNET_GDIM)
    chan = np.concatenate([np.cos(ang_c), np.sin(ang_c)], axis=0) * norm
    return (jnp.asarray(m1, F32).astype(BF16), jnp.asarray(m2, F32).astype(BF16),
            jnp.asarray(m3, F32).astype(BF16), jnp.asarray(chan, F32))


def _fnet_kernel(u_ref, wf_ref, m1_ref, m2_ref, m3_ref, chan_ref, out_ref,
                 y1r, y1i, y2r, y2i, mg_ref):
    R = RADIX
    T = SUBLANES
    W = FNET_HALF

    @pl.when(pl.program_id(0) == 0)
    def _():
        for g in range(FNET_GROUPS):
            mg_ref[g] = _dot_f32(chan_ref[...], wf_ref[g]).astype(BF16)

    for half in range(FNET_WIDTH // W):
        lanes = slice(half * W, (half + 1) * W)

        def stage1(sm, carry):
            blk = u_ref[:, sm, :, lanes].reshape(R * R, W)
            res = _dot(m1_ref[...], blk)
            y1r[:, :, sm, :, :] = res[:R * R].reshape(2, R, T, W)
            y1i[:, :, sm, :, :] = res[R * R:].reshape(2, R, T, W)
            return carry

        lax.fori_loop(0, R, stage1, 0)

        def stage2(i, carry):
            kah = i // R
            sl = i % R
            rhs = jnp.concatenate([y1r[kah, sl].reshape(R * T, W), y1i[kah, sl].reshape(R * T, W)],
                                  axis=0).astype(BF16)
            res = _dot(m2_ref[kah], rhs)
            y2r[kah, :, sl, :, :] = res[:R * T].reshape(R, T, W)
            y2i[kah, :, sl, :, :] = res[R * T:].reshape(R, T, W)
            return carry

        lax.fori_loop(0, 2 * R, stage2, 0)

        def stage3(i, carry):
            kah = i // R
            kb = i % R
            rhs = jnp.concatenate([y2r[kah, kb].reshape(R * T, W), y2i[kah, kb].reshape(R * T, W)],
                                  axis=0).astype(BF16)
            res = _dot(m3_ref[kah, kb], rhs)
            for j in range(W // FNET_GDIM):
                g = half * (W // FNET_GDIM) + j
                gl = slice(j * FNET_GDIM, (j + 1) * FNET_GDIM)
                z = jnp.concatenate([res[:R * T, gl], res[R * T:, gl]], axis=1).astype(BF16)
                o = _dot(z, mg_ref[g])
                out_ref[:, kb, kah, :, g * FNET_GDIM:(g + 1) * FNET_GDIM] = o.reshape(R, T, FNET_GDIM)
            return carry

        lax.fori_loop(0, 2 * R, stage3, 0)


def _fnet(u5, w_fnet, tables, batch):
    m1, m2, m3, chan = tables
    R, T, W = RADIX, SUBLANES, FNET_HALF
    return pl.pallas_call(
        _fnet_kernel,
        grid=(batch,),
        in_specs=[pl.BlockSpec((None, R, R, R, FNET_WIDTH), lambda b: (b, 0, 0, 0, 0)),
                  pl.BlockSpec((FNET_GROUPS, FNET_GDIM, FNET_GDIM), lambda b: (0, 0, 0)),
                  pl.BlockSpec(m1.shape, lambda b: (0, 0)),
                  pl.BlockSpec(m2.shape, lambda b: (0, 0, 0)),
                  pl.BlockSpec(m3.shape, lambda b: (0, 0, 0, 0)),
                  pl.BlockSpec(chan.shape, lambda b: (0, 0))],
        out_specs=pl.BlockSpec((None, R, R, 2, T, FNET_WIDTH), lambda b: (b, 0, 0, 0, 0, 0)),
        out_shape=jax.ShapeDtypeStruct((batch, R, R, 2, T, FNET_WIDTH), F32),
        scratch_shapes=[pltpu.VMEM((2, R, R, T, W), F32), pltpu.VMEM((2, R, R, T, W), F32),
                        pltpu.VMEM((2, R, R, T, W), F32), pltpu.VMEM((2, R, R, T, W), F32),
                        pltpu.VMEM((FNET_GROUPS, 2 * FNET_GDIM, FNET_GDIM), BF16)],
        compiler_params=pltpu.CompilerParams(dimension_semantics=("arbitrary",),
                                             vmem_limit_bytes=VMEM_LIMIT_BYTES),
        name="fnet",
    )(u5, w_fnet, m1, m2, m3, chan)


def _out_kernel(x_ref, yg_ref, uf_ref, zf_ref, w_ref, lng_ref, lnb_ref, o_ref):
    y_fnet = (uf_ref[...] * _silu(zf_ref[...].astype(F32))).astype(BF16)
    y = _dot(yg_ref[...], w_ref[:GLA_WIDTH, :]) + _dot(y_fnet, w_ref[GLA_WIDTH:, :])
    r = DEEPNORM_ALPHA * x_ref[...] + y
    mu = jnp.mean(r, axis=-1, keepdims=True)
    d = r - mu
    var = jnp.mean(d * d, axis=-1, keepdims=True)
    o_ref[...] = d * lax.rsqrt(var + LN_EPS) * lng_ref[...] + lnb_ref[...]


def _out_proj(x2d, y_gla, uf, zf, w_out, ln_g, ln_b):
    m = x2d.shape[0]
    row = lambda i: (i, 0)
    const = lambda i: (0, 0)
    return pl.pallas_call(
        _out_kernel,
        grid=(m // ROW_TILE,),
        in_specs=[pl.BlockSpec((ROW_TILE, D_MODEL), row),
                  pl.BlockSpec((ROW_TILE, GLA_WIDTH), row),
                  pl.BlockSpec((ROW_TILE, FNET_WIDTH), row),
                  pl.BlockSpec((ROW_TILE, FNET_WIDTH), row),
                  pl.BlockSpec((GLA_WIDTH + FNET_WIDTH, D_MODEL), const),
                  pl.BlockSpec((1, D_MODEL), const),
                  pl.BlockSpec((1, D_MODEL), const)],
        out_specs=pl.BlockSpec((ROW_TILE, D_MODEL), row),
        out_shape=jax.ShapeDtypeStruct((m, D_MODEL), F32),
        compiler_params=pltpu.CompilerParams(dimension_semantics=("arbitrary",),
                                             vmem_limit_bytes=VMEM_LIMIT_BYTES),
        name="out_proj",
    )(x2d, y_gla, uf, zf, w_out, ln_g, ln_b)


def _permute_w_in(w):
    kw, gw, fw, r = GLA_KEY_WIDTH, GLA_WIDTH, FNET_WIDTH, GATE_RANK
    o_q, o_k, o_v = 0, kw, 2 * kw
    o_gf = o_v + gw
    o_zg = o_gf + 2 * r
    o_u = o_zg + gw
    o_zf = o_u + fw
    gates = jnp.pad(w[:, o_gf:o_gf + 2 * r], ((0, 0), (0, GATE_PAD - 2 * r)))
    cols = [w[:, o_q:o_q + kw], w[:, o_k:o_k + kw], w[:, o_v:o_v + gw], w[:, o_zg:o_zg + gw],
            w[:, o_u:o_u + fw], w[:, o_zf:o_zf + fw], gates]
    return jnp.concatenate(cols, axis=1).astype(BF16)


def kernel(x, w_in, w_gate_up_fwd, b_gate_fwd, w_gate_up_bwd, b_gate_bwd, gla_norm_g,
           w_fnet, w_out, ln_g, ln_b):
    batch, seq, d = x.shape
    assert (seq, d) == (SEQ, D_MODEL) and w_in.shape[0] == DEPTH
    tables = _fnet_tables()
    for l in range(DEPTH):
        x2d = x.reshape(batch * seq, d)
        q, k, v, zg, u, zf, g = _projection(x2d, _permute_w_in(w_in[l]))

        wg = jnp.zeros((GATE_PAD, 2 * GLA_KEY_WIDTH), F32)
        wg = wg.at[:GATE_RANK, :GLA_KEY_WIDTH].set(w_gate_up_fwd[l])
        wg = wg.at[GATE_RANK:2 * GATE_RANK, GLA_KEY_WIDTH:].set(w_gate_up_bwd[l])
        bg = jnp.concatenate([b_gate_fwd[l], b_gate_bwd[l]])[None, :]
        y_gla = _gla(q, k, v, g, zg, wg, bg, gla_norm_g[l][None, :], batch)

        u5 = u.reshape(batch, RADIX, RADIX, RADIX, FNET_WIDTH)
        uf = _fnet(u5, w_fnet[l], tables, batch).reshape(batch * seq, FNET_WIDTH)

        out = _out_proj(x2d, y_gla, uf, zf, w_out[l].astype(BF16), ln_g[l][None, :], ln_b[l][None, :])
        x = out.reshape(batch, seq, d)
    return x
```

```python
import functools
import math

import numpy as np
import jax
import jax.numpy as jnp
from jax import lax
from jax.experimental import pallas as pl
from jax.experimental.pallas import tpu as pltpu

F32 = jnp.float32
BF16 = jnp.bfloat16

D_MODEL = 1024
SEQ = 4096
GLA_HEADS = 4
GLA_DK = 64
GLA_DV = 128
GLA_KEY_WIDTH = GLA_HEADS * GLA_DK
GLA_WIDTH = GLA_HEADS * GLA_DV
GATE_RANK = 16
GATE_TAU = 16.0
FNET_GROUPS = 4
FNET_GDIM = 128
FNET_WIDTH = FNET_GROUPS * FNET_GDIM
LN_EPS = 1e-5
RMS_EPS = 1e-6
DEPTH = 1
DEEPNORM_ALPHA = (2.0 * DEPTH) ** 0.25

LANES = 128
SUBLANES = 8
VMEM_LIMIT_BYTES = 56 * 1024 * 1024

ROW_TILE = 512
GLA_CHUNK = 256
GATE_PAD = LANES
RADIX = 16
FNET_HALF = 256
FNET_UNROLL = 8

_PROJ_COLS = (("q", GLA_KEY_WIDTH), ("k", GLA_KEY_WIDTH), ("v", GLA_WIDTH), ("zg", GLA_WIDTH),
              ("u", FNET_WIDTH), ("zf", FNET_WIDTH), ("g", GATE_PAD))
PROJ_WIDTH = sum(w for _, w in _PROJ_COLS)


def _dot(a, b):
    return jnp.dot(a, b, preferred_element_type=F32)


def _split2(x):
    hi = x.astype(BF16)
    lo = (x - hi.astype(F32)).astype(BF16)
    return hi, lo


def _split3(x):
    hi = x.astype(BF16)
    r = x - hi.astype(F32)
    mid = r.astype(BF16)
    lo = (r - mid.astype(F32)).astype(BF16)
    return hi, mid, lo


def _dot_f32(a, b):
    ah, al = _split2(a)
    bh, bl = _split2(b)
    return _dot(ah, bh) + _dot(ah, bl) + _dot(al, bh)


def _silu(z):
    return z / (1.0 + jnp.exp(-z))


def _proj_kernel(x_ref, w_ref, q_ref, k_ref, v_ref, zg_ref, u_ref, zf_ref, g_ref):
    xb = x_ref[...].astype(BF16)
    outs = (q_ref, k_ref, v_ref, zg_ref, u_ref, zf_ref, g_ref)
    off = 0
    for (_, width), o_ref in zip(_PROJ_COLS, outs):
        o_ref[...] = _dot(xb, w_ref[:, off:off + width]).astype(o_ref.dtype)
        off += width


def _projection(x2d, w_perm):
    m = x2d.shape[0]
    grid = (m // ROW_TILE,)
    row = lambda i: (i, 0)
    out_shape = [jax.ShapeDtypeStruct((m, w), F32 if name == "g" else BF16) for name, w in _PROJ_COLS]
    out_specs = [pl.BlockSpec((ROW_TILE, w), row) for _, w in _PROJ_COLS]
    return pl.pallas_call(
        _proj_kernel,
        grid=grid,
        in_specs=[pl.BlockSpec((ROW_TILE, D_MODEL), row),
                  pl.BlockSpec((D_MODEL, PROJ_WIDTH), lambda i: (0, 0))],
        out_specs=out_specs,
        out_shape=out_shape,
        compiler_params=pltpu.CompilerParams(dimension_semantics=("arbitrary",),
                                             vmem_limit_bytes=VMEM_LIMIT_BYTES),
        name="in_proj",
    )(x2d, w_perm)


def _gla_kernel(q_ref, k_ref, v_ref, g_ref, wg_ref, bg_ref, gam_ref, y_ref,
                oacc_ref, qd_ref, upd_ref, dec_ref):
    C = GLA_CHUNK
    KW = GLA_KEY_WIDTH
    n_chunks = SEQ // C
    row_i = lax.broadcasted_iota(jnp.int32, (C, C), 0)
    col_i = lax.broadcasted_iota(jnp.int32, (C, C), 1)
    lower = row_i >= col_i
    upper = row_i <= col_i
    lane_head = lax.broadcasted_iota(jnp.int32, (C, KW), 1) // GLA_DK
    state_head = lax.broadcasted_iota(jnp.int32, (KW, GLA_DV), 0) // GLA_DK
    gate_lane = lax.broadcasted_iota(jnp.int32, (C, GATE_PAD), 1)
    gate_lo_lanes = (gate_lane >= 2 * GATE_RANK) & (gate_lane < 4 * GATE_RANK)

    def intra(n, carry):
        rows = pl.ds(pl.multiple_of(n * C, C), C)
        g4 = g_ref[rows, :]
        g_hi = g4.astype(BF16)
        g_lo = (g4 - g_hi.astype(F32)).astype(BF16)
        z = _dot(jnp.where(gate_lo_lanes, g_lo, g_hi), wg_ref[...]) + bg_ref[...]
        log_a = (jnp.minimum(z, 0.0) - jnp.log(1.0 + jnp.exp(-jnp.abs(z)))) * (1.0 / GATE_TAU)
        hi = log_a.astype(BF16)
        mid = (log_a - hi.astype(F32)).astype(BF16)
        lo_t = lower.astype(BF16)
        up_t = upper.astype(BF16)
        b_f = _dot(lo_t, hi[:, :KW]) + _dot(lo_t, mid[:, :KW])
        b_b = _dot(up_t, hi[:, KW:]) + _dot(up_t, mid[:, KW:])
        e_f = b_f[C - 1:C, :]
        e_b = b_b[0:1, :]
        dec_ref[n, 0] = jnp.broadcast_to(jnp.exp(e_f), (SUBLANES, KW))
        dec_ref[n, 1] = jnp.broadcast_to(jnp.exp(e_b), (SUBLANES, KW))

        qf = q_ref[rows, :].astype(F32) * (GLA_DK ** -0.5)
        kf = k_ref[rows, :].astype(F32)
        qd_f = (qf * jnp.exp(b_f)).astype(BF16)
        qd_b = (qf * jnp.exp(b_b)).astype(BF16)
        qd_ref[rows, :KW] = qd_f
        qd_ref[rows, KW:] = qd_b
        ki_f = (kf * jnp.exp(-b_f)).astype(BF16)
        ki_b = (kf * jnp.exp(-b_b)).astype(BF16)
        kd = jnp.concatenate([kf * jnp.exp(e_f - b_f), kf * jnp.exp(e_b - b_b)], axis=1)
        kd_t = jnp.transpose(kd).astype(BF16)
        vb = v_ref[rows, :]
        nt = (((1,), (1,)), ((), ()))
        for h in range(GLA_HEADS):
            vcols = slice(h * GLA_DV, (h + 1) * GLA_DV)
            in_head = lane_head == h
            s_f = lax.dot_general(jnp.where(in_head, qd_f, jnp.zeros_like(qd_f)), ki_f, nt,
                                  preferred_element_type=F32)
            s_b = lax.dot_general(jnp.where(in_head, qd_b, jnp.zeros_like(qd_b)), ki_b, nt,
                                  preferred_element_type=F32)
            p = (jnp.where(lower, s_f, 0.0) + jnp.where(upper, s_b, 0.0)).astype(BF16)
            lhs = jnp.concatenate([p, kd_t[h * GLA_DK:(h + 1) * GLA_DK, :],
                                   kd_t[KW + h * GLA_DK:KW + (h + 1) * GLA_DK, :]], axis=0)
            r = _dot(lhs, vb[:, vcols])
            oacc_ref[rows, vcols] = r[:C]
            upd_ref[n, 0, h * GLA_DK:(h + 1) * GLA_DK, :] = r[C:C + GLA_DK]
            upd_ref[n, 1, h * GLA_DK:(h + 1) * GLA_DK, :] = r[C + GLA_DK:]
        return carry

    lax.fori_loop(0, n_chunks, intra, 0)

    def expand(state):
        sb = state.astype(BF16)
        return jnp.concatenate([jnp.where(state_head == h, sb, jnp.zeros_like(sb))
                                for h in range(GLA_HEADS)], axis=1)

    def advance(n, d, state, qcols):
        rows = pl.ds(pl.multiple_of(n * C, C), C)
        oacc_ref[rows, :] += _dot(qd_ref[rows, qcols], expand(state))
        decay_t = jnp.transpose(jnp.broadcast_to(dec_ref[n, d][0:1, :], (LANES, KW)))
        return state * decay_t + upd_ref[n, d]

    def finalize(n):
        rows = pl.ds(pl.multiple_of(n * C, C), C)
        for h in range(GLA_HEADS):
            vcols = slice(h * GLA_DV, (h + 1) * GLA_DV)
            o_h = oacc_ref[rows, vcols]
            ms = jnp.mean(o_h * o_h, axis=-1, keepdims=True)
            y_ref[rows, vcols] = (o_h * lax.rsqrt(ms + RMS_EPS) * gam_ref[:, vcols]).astype(y_ref.dtype)

    def inter(j, states, done):
        s_f, s_b = states
        m = n_chunks - 1 - j
        s_f = advance(j, 0, s_f, slice(0, KW))
        s_b = advance(m, 1, s_b, slice(KW, 2 * KW))
        if done:
            finalize(j)
            finalize(m)
        return s_f, s_b

    state0 = jnp.zeros((KW, GLA_DV), F32)
    states = lax.fori_loop(0, n_chunks // 2, lambda j, st: inter(j, st, False), (state0, state0))
    lax.fori_loop(n_chunks // 2, n_chunks, lambda j, st: inter(j, st, True), states)


def _gla(q, k, v, g, wg, bg, gamma, batch):
    tok = lambda b: (b, 0)
    const = lambda b: (0, 0)
    n_chunks = SEQ // GLA_CHUNK
    return pl.pallas_call(
        _gla_kernel,
        grid=(batch,),
        in_specs=[pl.BlockSpec((SEQ, GLA_KEY_WIDTH), tok),
                  pl.BlockSpec((SEQ, GLA_KEY_WIDTH), tok),
                  pl.BlockSpec((SEQ, GLA_WIDTH), tok),
                  pl.BlockSpec((SEQ, GATE_PAD), tok),
                  pl.BlockSpec((GATE_PAD, 2 * GLA_KEY_WIDTH), const),
                  pl.BlockSpec((1, 2 * GLA_KEY_WIDTH), const),
                  pl.BlockSpec((1, GLA_WIDTH), const)],
        out_specs=pl.BlockSpec((SEQ, GLA_WIDTH), tok),
        out_shape=jax.ShapeDtypeStruct((batch * SEQ, GLA_WIDTH), BF16),
        scratch_shapes=[pltpu.VMEM((SEQ, GLA_WIDTH), F32),
                        pltpu.VMEM((SEQ, 2 * GLA_KEY_WIDTH), BF16),
                        pltpu.VMEM((n_chunks, 2, GLA_KEY_WIDTH, GLA_DV), F32),
                        pltpu.VMEM((n_chunks, 2, SUBLANES, GLA_KEY_WIDTH), F32)],
        compiler_params=pltpu.CompilerParams(dimension_semantics=("arbitrary",),
                                             vmem_limit_bytes=VMEM_LIMIT_BYTES),
        name="gla",
    )(q, k, v, g, wg, bg, gamma)


def _fnet_tables():
    R = RADIX
    n = SEQ
    a = np.arange(R)
    lo8 = np.arange(SUBLANES)
    eye8 = np.eye(SUBLANES)

    ka = (8 * np.arange(2)[:, None] + lo8[None, :])
    ang = 2 * np.pi * (256 * a[None, None, :] * ka[:, :, None]) / n
    eye16 = np.eye(R)
    def s1(trig):
        t = trig(ang)
        m = t[:, None, :, :, None] * eye16[None, :, None, None, :]
        return m.reshape(2 * R * SUBLANES, R * R)
    m1 = np.concatenate([s1(np.cos), -s1(np.sin)], axis=0)

    def complex_block(theta):
        c = np.cos(theta)
        s = -np.sin(theta)
        def kron(t):
            m = t[:, :, :, None] * eye8[None, :, None, :]
            return m.reshape(t.shape[0] * SUBLANES, t.shape[2] * SUBLANES)
        mr, mi = kron(c), kron(s)
        return np.block([[mr, -mi], [mi, mr]])

    m2 = []
    for kah in range(2):
        k_a = 8 * kah + lo8
        expo = 16 * a[None, None, :] * k_a[None, :, None] + 256 * a[None, None, :] * a[:, None, None]
        m2.append(complex_block(2 * np.pi * expo / n))
    m2 = np.stack(m2)

    m3 = np.zeros((2, R, 2 * R * SUBLANES, 2 * R * SUBLANES))
    for kah in range(2):
        k_a = 8 * kah + lo8
        for kb in range(R):
            expo = (a[None, None, :] * k_a[None, :, None] + 16 * a[None, None, :] * kb
                    + 256 * a[None, None, :] * a[:, None, None])
            m3[kah, kb] = complex_block(2 * np.pi * expo / n)

    c = np.arange(FNET_GDIM)
    ang_c = 2 * np.pi * np.outer(c, c) / FNET_GDIM
    norm = 1.0 / math.sqrt(SEQ * FNET_GDIM)
    chan = np.concatenate([np.cos(ang_c), np.sin(ang_c)], axis=0) * norm
    return (jnp.asarray(m1, F32).astype(BF16), jnp.asarray(m2, F32).astype(BF16),
            jnp.asarray(m3, F32).astype(BF16), jnp.asarray(chan, F32))


def _fnet_kernel(u_ref, wf_ref, m1_ref, m2_ref, m3_ref, chan_ref, out_ref,
                 y1r, y1i, y2r, y2i, mg_ref):
    R = RADIX
    T = SUBLANES
    W = FNET_HALF

    @pl.when(pl.program_id(0) == 0)
    def _():
        for g in range(FNET_GROUPS):
            mg_ref[g] = _dot_f32(chan_ref[...], wf_ref[g]).astype(BF16)

    for half in range(FNET_WIDTH // W):
        lanes = slice(half * W, (half + 1) * W)

        def stage1(sm, carry):
            blk = u_ref[:, sm, :, lanes].reshape(R * R, W)
            res = _dot(m1_ref[...], blk)
            y1r[:, :, sm, :, :] = res[:R * R].reshape(2, R, T, W)
            y1i[:, :, sm, :, :] = res[R * R:].reshape(2, R, T, W)
            return carry

        lax.fori_loop(0, R, stage1, 0, unroll=FNET_UNROLL)

        def stage2(i, carry):
            kah = i // R
            sl = i % R
            rhs = jnp.concatenate([y1r[kah, sl].reshape(R * T, W), y1i[kah, sl].reshape(R * T, W)],
                                  axis=0).astype(BF16)
            res = _dot(m2_ref[kah], rhs)
            y2r[kah, :, sl, :, :] = res[:R * T].reshape(R, T, W)
            y2i[kah, :, sl, :, :] = res[R * T:].reshape(R, T, W)
            return carry

        lax.fori_loop(0, 2 * R, stage2, 0, unroll=FNET_UNROLL)

        def stage3(i, carry):
            kah = i // R
            kb = i % R
            rhs = jnp.concatenate([y2r[kah, kb].reshape(R * T, W), y2i[kah, kb].reshape(R * T, W)],
                                  axis=0).astype(BF16)
            res = _dot(m3_ref[kah, kb], rhs)
            for j in range(W // FNET_GDIM):
                g = half * (W // FNET_GDIM) + j
                gl = slice(j * FNET_GDIM, (j + 1) * FNET_GDIM)
                z = jnp.concatenate([res[:R * T, gl], res[R * T:, gl]], axis=1).astype(BF16)
                o = _dot(z, mg_ref[g])
                out_ref[:, kb, kah, :, g * FNET_GDIM:(g + 1) * FNET_GDIM] = o.reshape(R, T, FNET_GDIM)
            return carry

        lax.fori_loop(0, 2 * R, stage3, 0, unroll=FNET_UNROLL)


def _fnet(u5, w_fnet, tables, batch):
    m1, m2, m3, chan = tables
    R, T, W = RADIX, SUBLANES, FNET_HALF
    return pl.pallas_call(
        _fnet_kernel,
        grid=(batch,),
        in_specs=[pl.BlockSpec((None, R, R, R, FNET_WIDTH), lambda b: (b, 0, 0, 0, 0)),
                  pl.BlockSpec((FNET_GROUPS, FNET_GDIM, FNET_GDIM), lambda b: (0, 0, 0)),
                  pl.BlockSpec(m1.shape, lambda b: (0, 0)),
                  pl.BlockSpec(m2.shape, lambda b: (0, 0, 0)),
                  pl.BlockSpec(m3.shape, lambda b: (0, 0, 0, 0)),
                  pl.BlockSpec(chan.shape, lambda b: (0, 0))],
        out_specs=pl.BlockSpec((None, R, R, 2, T, FNET_WIDTH), lambda b: (b, 0, 0, 0, 0, 0)),
        out_shape=jax.ShapeDtypeStruct((batch, R, R, 2, T, FNET_WIDTH), F32),
        scratch_shapes=[pltpu.VMEM((2, R, R, T, W), F32), pltpu.VMEM((2, R, R, T, W), F32),
                        pltpu.VMEM((2, R, R, T, W), F32), pltpu.VMEM((2, R, R, T, W), F32),
                        pltpu.VMEM((FNET_GROUPS, 2 * FNET_GDIM, FNET_GDIM), BF16)],
        compiler_params=pltpu.CompilerParams(dimension_semantics=("arbitrary",),
                                             vmem_limit_bytes=VMEM_LIMIT_BYTES),
        name="fnet",
    )(u5, w_fnet, m1, m2, m3, chan)


def _out_kernel(x_ref, og_ref, zg_ref, uf_ref, zf_ref, w_ref, lng_ref, lnb_ref, o_ref):
    y_gla = (og_ref[...].astype(F32) * _silu(zg_ref[...].astype(F32))).astype(BF16)
    y_fnet = (uf_ref[...] * _silu(zf_ref[...].astype(F32))).astype(BF16)
    y = _dot(y_gla, w_ref[:GLA_WIDTH, :]) + _dot(y_fnet, w_ref[GLA_WIDTH:, :])
    r = DEEPNORM_ALPHA * x_ref[...] + y
    mu = jnp.mean(r, axis=-1, keepdims=True)
    d = r - mu
    var = jnp.mean(d * d, axis=-1, keepdims=True)
    o_ref[...] = d * lax.rsqrt(var + LN_EPS) * lng_ref[...] + lnb_ref[...]


def _out_proj(x2d, o_gla, zg, uf, zf, w_out, ln_g, ln_b):
    m = x2d.shape[0]
    row = lambda i: (i, 0)
    const = lambda i: (0, 0)
    return pl.pallas_call(
        _out_kernel,
        grid=(m // ROW_TILE,),
        in_specs=[pl.BlockSpec((ROW_TILE, D_MODEL), row),
                  pl.BlockSpec((ROW_TILE, GLA_WIDTH), row),
                  pl.BlockSpec((ROW_TILE, GLA_WIDTH), row),
                  pl.BlockSpec((ROW_TILE, FNET_WIDTH), row),
                  pl.BlockSpec((ROW_TILE, FNET_WIDTH), row),
                  pl.BlockSpec((GLA_WIDTH + FNET_WIDTH, D_MODEL), const),
                  pl.BlockSpec((1, D_MODEL), const),
                  pl.BlockSpec((1, D_MODEL), const)],
        out_specs=pl.BlockSpec((ROW_TILE, D_MODEL), row),
        out_shape=jax.ShapeDtypeStruct((m, D_MODEL), F32),
        compiler_params=pltpu.CompilerParams(dimension_semantics=("arbitrary",),
                                             vmem_limit_bytes=VMEM_LIMIT_BYTES),
        name="out_proj",
    )(x2d, o_gla, zg, uf, zf, w_out, ln_g, ln_b)


def _permute_w_in(w):
    kw, gw, fw, r = GLA_KEY_WIDTH, GLA_WIDTH, FNET_WIDTH, GATE_RANK
    o_q, o_k, o_v = 0, kw, 2 * kw
    o_gf = o_v + gw
    o_zg = o_gf + 2 * r
    o_u = o_zg + gw
    o_zf = o_u + fw
    gates = jnp.concatenate([w[:, o_gf:o_gf + 2 * r]] * (GATE_PAD // (2 * r)), axis=1)
    cols = [w[:, o_q:o_q + kw], w[:, o_k:o_k + kw], w[:, o_v:o_v + gw], w[:, o_zg:o_zg + gw],
            w[:, o_u:o_u + fw], w[:, o_zf:o_zf + fw], gates]
    return jnp.concatenate(cols, axis=1).astype(BF16)


def kernel(x, w_in, w_gate_up_fwd, b_gate_fwd, w_gate_up_bwd, b_gate_bwd, gla_norm_g,
           w_fnet, w_out, ln_g, ln_b):
    batch, seq, d = x.shape
    assert (seq, d) == (SEQ, D_MODEL) and w_in.shape[0] == DEPTH
    tables = _fnet_tables()
    for l in range(DEPTH):
        x2d = x.reshape(batch * seq, d)
        q, k, v, zg, u, zf, g = _projection(x2d, _permute_w_in(w_in[l]))

        zero = jnp.zeros((GATE_RANK, GLA_KEY_WIDTH), F32)
        w_bd = jnp.block([[w_gate_up_fwd[l], zero], [zero, w_gate_up_bwd[l]]])
        w_hi = w_bd.astype(BF16)
        w_lo = (w_bd - w_hi.astype(F32)).astype(BF16)
        wg = jnp.concatenate([w_hi, w_hi, w_lo, jnp.zeros_like(w_hi)], axis=0)
        bg = jnp.concatenate([b_gate_fwd[l], b_gate_bwd[l]])[None, :]
        o_gla = _gla(q, k, v, g, wg, bg, gla_norm_g[l][None, :], batch)

        u5 = u.reshape(batch, RADIX, RADIX, RADIX, FNET_WIDTH)
        uf = _fnet(u5, w_fnet[l], tables, batch).reshape(batch * seq, FNET_WIDTH)

        out = _out_proj(x2d, o_gla, zg, uf, zf, w_out[l].astype(BF16), ln_g[l][None, :],
                        ln_b[l][None, :])
        x = out.reshape(batch, seq, d)
    return x
```

```python
import functools
import math

import numpy as np
import jax
import jax.numpy as jnp
from jax import lax
from jax.experimental import pallas as pl
from jax.experimental.pallas import tpu as pltpu

F32 = jnp.float32
BF16 = jnp.bfloat16

D_MODEL = 1024
SEQ = 4096
GLA_HEADS = 4
GLA_DK = 64
GLA_DV = 128
GLA_KEY_WIDTH = GLA_HEADS * GLA_DK
GLA_WIDTH = GLA_HEADS * GLA_DV
GATE_RANK = 16
GATE_TAU = 16.0
FNET_GROUPS = 4
FNET_GDIM = 128
FNET_WIDTH = FNET_GROUPS * FNET_GDIM
LN_EPS = 1e-5
RMS_EPS = 1e-6
DEPTH = 1
DEEPNORM_ALPHA = (2.0 * DEPTH) ** 0.25
LOG2_E = 1.0 / math.log(2.0)

LANES = 128
SUBLANES = 8
VMEM_LIMIT_BYTES = 60 * 1024 * 1024

ROW_TILE = 512
GLA_CHUNK = 256
GATE_PAD = LANES
RADIX = 16
FNET_HALF = 256
GLA_UNROLL = 2
FNET_UNROLL = 8

_PROJ_COLS = (("q", GLA_KEY_WIDTH), ("k", GLA_KEY_WIDTH), ("v", GLA_WIDTH), ("zg", GLA_WIDTH),
              ("u", FNET_WIDTH), ("zf", FNET_WIDTH), ("g", GATE_PAD))
PROJ_WIDTH = sum(w for _, w in _PROJ_COLS)


def _dot(a, b):
    return jnp.dot(a, b, preferred_element_type=F32)


def _split2(x):
    hi = x.astype(BF16)
    lo = (x - hi.astype(F32)).astype(BF16)
    return hi, lo


def _split3(x):
    hi = x.astype(BF16)
    r = x - hi.astype(F32)
    mid = r.astype(BF16)
    lo = (r - mid.astype(F32)).astype(BF16)
    return hi, mid, lo


def _dot_f32(a, b):
    ah, al = _split2(a)
    bh, bl = _split2(b)
    return _dot(ah, bh) + _dot(ah, bl) + _dot(al, bh)


def _silu(z):
    h = 0.5 * z
    return h + h * jnp.tanh(h)


def _proj_kernel(x_ref, w_ref, q_ref, k_ref, v_ref, zg_ref, u_ref, zf_ref, g_ref):
    xb = x_ref[...].astype(BF16)
    outs = (q_ref, k_ref, v_ref, zg_ref, u_ref, zf_ref, g_ref)
    off = 0
    for (name, width), o_ref in zip(_PROJ_COLS, outs):
        r = _dot(xb, w_ref[:, off:off + width])
        if name == "g":
            hi, lo = _split2(r)
            lane = lax.broadcasted_iota(jnp.int32, r.shape, 1)
            r = jnp.where((lane >= 2 * GATE_RANK) & (lane < 4 * GATE_RANK), lo, hi)
        o_ref[...] = r.astype(o_ref.dtype)
        off += width


def _projection(x2d, w_perm):
    m = x2d.shape[0]
    grid = (m // ROW_TILE,)
    row = lambda i: (i, 0)
    out_shape = [jax.ShapeDtypeStruct((m, w), BF16) for _, w in _PROJ_COLS]
    out_specs = [pl.BlockSpec((ROW_TILE, w), row) for _, w in _PROJ_COLS]
    return pl.pallas_call(
        _proj_kernel,
        grid=grid,
        in_specs=[pl.BlockSpec((ROW_TILE, D_MODEL), row),
                  pl.BlockSpec((D_MODEL, PROJ_WIDTH), lambda i: (0, 0))],
        out_specs=out_specs,
        out_shape=out_shape,
        compiler_params=pltpu.CompilerParams(dimension_semantics=("arbitrary",),
                                             vmem_limit_bytes=VMEM_LIMIT_BYTES),
        name="in_proj",
    )(x2d, w_perm)


def _gla_kernel(q_ref, k_ref, v_ref, g_ref, wg_ref, bg_ref, gam_ref, y_ref,
                oacc_ref, qd_ref, ki_ref, kdt_ref, upd_ref, dec_ref, dect_ref):
    C = GLA_CHUNK
    KW = GLA_KEY_WIDTH
    n_chunks = SEQ // C
    row_i = lax.broadcasted_iota(jnp.int32, (C, C), 0)
    col_i = lax.broadcasted_iota(jnp.int32, (C, C), 1)
    lower = row_i >= col_i
    upper = row_i <= col_i
    lane_head = lax.broadcasted_iota(jnp.int32, (C, KW), 1) // GLA_DK
    state_head = lax.broadcasted_iota(jnp.int32, (KW, GLA_DV), 0) // GLA_DK
    pair_ones = (lax.broadcasted_iota(jnp.int32, (2 * GLA_DV, 2 * GLA_DV), 0) // GLA_DV ==
                 lax.broadcasted_iota(jnp.int32, (2 * GLA_DV, 2 * GLA_DV), 1) // GLA_DV).astype(BF16)

    def prep(n, carry):
        rows = pl.ds(pl.multiple_of(n * C, C), C)
        z2 = _dot(g_ref[rows, :], wg_ref[...]) + bg_ref[...]
        tau_log2_a = jnp.minimum(z2, 0.0) - jnp.log2(1.0 + jnp.exp2(-jnp.abs(z2)))
        hi = tau_log2_a.astype(BF16)
        mid = (tau_log2_a - hi.astype(F32)).astype(BF16)
        lo_t = jnp.where(lower, 1.0 / GATE_TAU, 0.0).astype(BF16)
        up_t = jnp.where(upper, 1.0 / GATE_TAU, 0.0).astype(BF16)
        b_f = _dot(lo_t, hi[:, :KW]) + _dot(lo_t, mid[:, :KW])
        b_b = _dot(up_t, hi[:, KW:]) + _dot(up_t, mid[:, KW:])
        e_f = b_f[C - 1:C, :]
        e_b = b_b[0:1, :]
        dec_ref[n, 0] = jnp.broadcast_to(jnp.exp2(e_f), (SUBLANES, KW))
        dec_ref[n, 1] = jnp.broadcast_to(jnp.exp2(e_b), (SUBLANES, KW))

        qf = q_ref[rows, :].astype(F32)
        kf = k_ref[rows, :].astype(F32)
        qd_ref[rows, :KW] = (qf * jnp.exp2(b_f)).astype(BF16)
        qd_ref[rows, KW:] = (qf * jnp.exp2(b_b)).astype(BF16)
        ki_ref[rows, :KW] = (kf * jnp.exp2(-b_f)).astype(BF16)
        ki_ref[rows, KW:] = (kf * jnp.exp2(-b_b)).astype(BF16)
        kd = jnp.concatenate([kf * jnp.exp2(e_f - b_f), kf * jnp.exp2(e_b - b_b)], axis=1)
        kdt_ref[n] = jnp.transpose(kd).astype(BF16)
        return carry

    lax.fori_loop(0, n_chunks, prep, 0, unroll=GLA_UNROLL)

    def attend(n, carry):
        rows = pl.ds(pl.multiple_of(n * C, C), C)
        nt = (((1,), (1,)), ((), ()))
        qd_f = qd_ref[rows, :KW]
        qd_b = qd_ref[rows, KW:]
        probs = []
        for h in range(GLA_HEADS):
            in_head = lane_head == h
            s_f = lax.dot_general(jnp.where(in_head, qd_f, jnp.zeros_like(qd_f)), ki_ref[rows, :KW],
                                  nt, preferred_element_type=F32)
            s_b = lax.dot_general(jnp.where(in_head, qd_b, jnp.zeros_like(qd_b)), ki_ref[rows, KW:],
                                  nt, preferred_element_type=F32)
            probs.append((jnp.where(lower, s_f, 0.0) + jnp.where(upper, s_b, 0.0)).astype(BF16))
        for h in range(GLA_HEADS):
            vcols = slice(h * GLA_DV, (h + 1) * GLA_DV)
            lhs = jnp.concatenate([probs[h], kdt_ref[n, h * GLA_DK:(h + 1) * GLA_DK, :],
                                   kdt_ref[n, KW + h * GLA_DK:KW + (h + 1) * GLA_DK, :]], axis=0)
            r = _dot(lhs, v_ref[rows, vcols])
            oacc_ref[rows, vcols] = r[:C]
            upd_ref[n, 0, h * GLA_DK:(h + 1) * GLA_DK, :] = r[C:C + GLA_DK]
            upd_ref[n, 1, h * GLA_DK:(h + 1) * GLA_DK, :] = r[C + GLA_DK:]
        for d in range(2):
            dect_ref[n, d] = jnp.transpose(jnp.broadcast_to(dec_ref[n, d][0:1, :], (LANES, KW)))
        return carry

    lax.fori_loop(0, n_chunks, attend, 0, unroll=GLA_UNROLL)

    def expand(state):
        sb = state.astype(BF16)
        return jnp.concatenate([jnp.where(state_head == h, sb, jnp.zeros_like(sb))
                                for h in range(GLA_HEADS)], axis=1)

    def advance(n, d, state, qcols):
        rows = pl.ds(pl.multiple_of(n * C, C), C)
        oacc_ref[rows, :] += _dot(qd_ref[rows, qcols], expand(state))
        return state * dect_ref[n, d] + upd_ref[n, d]

    def finalize(n):
        rows = pl.ds(pl.multiple_of(n * C, C), C)
        for pair in range(GLA_HEADS // 2):
            cols = slice(2 * pair * GLA_DV, 2 * (pair + 1) * GLA_DV)
            o_p = oacc_ref[rows, cols]
            ms = _dot((o_p * o_p).astype(BF16), pair_ones) * (1.0 / GLA_DV)
            y_ref[rows, cols] = (o_p * lax.rsqrt(ms + RMS_EPS) * gam_ref[:, cols]).astype(y_ref.dtype)

    def inter(j, states, done):
        s_f, s_b = states
        m = n_chunks - 1 - j
        s_f = advance(j, 0, s_f, slice(0, KW))
        s_b = advance(m, 1, s_b, slice(KW, 2 * KW))
        if done:
            finalize(j)
            finalize(m)
        return s_f, s_b

    state0 = jnp.zeros((KW, GLA_DV), F32)
    states = lax.fori_loop(0, n_chunks // 2, lambda j, st: inter(j, st, False), (state0, state0),
                           unroll=GLA_UNROLL)
    lax.fori_loop(n_chunks // 2, n_chunks, lambda j, st: inter(j, st, True), states,
                  unroll=GLA_UNROLL)


def _gla(q, k, v, g, wg, bg, gamma, batch):
    tok = lambda b: (b, 0)
    const = lambda b: (0, 0)
    n_chunks = SEQ // GLA_CHUNK
    return pl.pallas_call(
        _gla_kernel,
        grid=(batch,),
        in_specs=[pl.BlockSpec((SEQ, GLA_KEY_WIDTH), tok),
                  pl.BlockSpec((SEQ, GLA_KEY_WIDTH), tok),
                  pl.BlockSpec((SEQ, GLA_WIDTH), tok),
                  pl.BlockSpec((SEQ, GATE_PAD), tok),
                  pl.BlockSpec((GATE_PAD, 2 * GLA_KEY_WIDTH), const),
                  pl.BlockSpec((1, 2 * GLA_KEY_WIDTH), const),
                  pl.BlockSpec((1, GLA_WIDTH), const)],
        out_specs=pl.BlockSpec((SEQ, GLA_WIDTH), tok),
        out_shape=jax.ShapeDtypeStruct((batch * SEQ, GLA_WIDTH), BF16),
        scratch_shapes=[pltpu.VMEM((SEQ, GLA_WIDTH), F32),
                        pltpu.VMEM((SEQ, 2 * GLA_KEY_WIDTH), BF16),
                        pltpu.VMEM((SEQ, 2 * GLA_KEY_WIDTH), BF16),
                        pltpu.VMEM((n_chunks, 2 * GLA_KEY_WIDTH, GLA_CHUNK), BF16),
                        pltpu.VMEM((n_chunks, 2, GLA_KEY_WIDTH, GLA_DV), F32),
                        pltpu.VMEM((n_chunks, 2, SUBLANES, GLA_KEY_WIDTH), F32),
                        pltpu.VMEM((n_chunks, 2, GLA_KEY_WIDTH, LANES), F32)],
        compiler_params=pltpu.CompilerParams(dimension_semantics=("arbitrary",),
                                             vmem_limit_bytes=VMEM_LIMIT_BYTES),
        name="gla",
    )(q, k, v, g, wg, bg, gamma)


def _fnet_tables():
    R = RADIX
    n = SEQ
    a = np.arange(R)
    lo8 = np.arange(SUBLANES)
    eye8 = np.eye(SUBLANES)

    ka = (8 * np.arange(2)[:, None] + lo8[None, :])
    ang = 2 * np.pi * (256 * a[None, None, :] * ka[:, :, None]) / n
    eye16 = np.eye(R)
    def s1(trig):
        t = trig(ang)
        m = t[:, None, :, :, None] * eye16[None, :, None, None, :]
        return m.reshape(2 * R * SUBLANES, R * R)
    m1 = np.concatenate([s1(np.cos), -s1(np.sin)], axis=0)

    def complex_block(theta):
        c = np.cos(theta)
        s = -np.sin(theta)
        def kron(t):
            m = t[:, :, :, None] * eye8[None, :, None, :]
            return m.reshape(t.shape[0] * SUBLANES, t.shape[2] * SUBLANES)
        mr, mi = kron(c), kron(s)
        return np.block([[mr, -mi], [mi, mr]])

    m2 = []
    for kah in range(2):
        k_a = 8 * kah + lo8
        expo = 16 * a[None, None, :] * k_a[None, :, None] + 256 * a[None, None, :] * a[:, None, None]
        m2.append(complex_block(2 * np.pi * expo / n))
    m2 = np.stack(m2)

    m3 = np.zeros((2, R, 2 * R * SUBLANES, 2 * R * SUBLANES))
    for kah in range(2):
        k_a = 8 * kah + lo8
        for kb in range(R):
            expo = (a[None, None, :] * k_a[None, :, None] + 16 * a[None, None, :] * kb
                    + 256 * a[None, None, :] * a[:, None, None])
            m3[kah, kb] = complex_block(2 * np.pi * expo / n)

    c = np.arange(FNET_GDIM)
    ang_c = 2 * np.pi * np.outer(c, c) / FNET_GDIM
    norm = 1.0 / math.sqrt(SEQ * FNET_GDIM)
    chan = np.concatenate([np.cos(ang_c), np.sin(ang_c)], axis=0) * norm
    return (jnp.asarray(m1, F32).astype(BF16), jnp.asarray(m2, F32).astype(BF16),
            jnp.asarray(m3, F32).astype(BF16), jnp.asarray(chan, F32))


def _fnet_kernel(u_ref, wf_ref, m1_ref, m2_ref, m3_ref, chan_ref, out_ref,
                 y1r, y1i, y2r, y2i, mg_ref):
    R = RADIX
    T = SUBLANES
    W = FNET_HALF

    @pl.when(pl.program_id(0) == 0)
    def _():
        for g in range(FNET_GROUPS):
            mg_ref[g] = _dot_f32(chan_ref[...], wf_ref[g]).astype(BF16)

    for half in range(FNET_WIDTH // W):
        lanes = slice(half * W, (half + 1) * W)

        def stage1(sm, carry):
            blk = u_ref[:, sm, :, lanes].reshape(R * R, W)
            res = _dot(m1_ref[...], blk)
            y1r[:, :, sm, :, :] = res[:R * R].reshape(2, R, T, W)
            y1i[:, :, sm, :, :] = res[R * R:].reshape(2, R, T, W)
            return carry

        lax.fori_loop(0, R, stage1, 0, unroll=FNET_UNROLL)

        def stage2(i, carry):
            kah = i // R
            sl = i % R
            rhs = jnp.concatenate([y1r[kah, sl].reshape(R * T, W), y1i[kah, sl].reshape(R * T, W)],
                                  axis=0).astype(BF16)
            res = _dot(m2_ref[kah], rhs)
            y2r[kah, :, sl, :, :] = res[:R * T].reshape(R, T, W)
            y2i[kah, :, sl, :, :] = res[R * T:].reshape(R, T, W)
            return carry

        lax.fori_loop(0, 2 * R, stage2, 0, unroll=FNET_UNROLL)

        def stage3(i, carry):
            kah = i // R
            kb = i % R
            rhs = jnp.concatenate([y2r[kah, kb].reshape(R * T, W), y2i[kah, kb].reshape(R * T, W)],
                                  axis=0).astype(BF16)
            res = _dot(m3_ref[kah, kb], rhs)
            for j in range(W // FNET_GDIM):
                g = half * (W // FNET_GDIM) + j
                gl = slice(j * FNET_GDIM, (j + 1) * FNET_GDIM)
                z = jnp.concatenate([res[:R * T, gl], res[R * T:, gl]], axis=1).astype(BF16)
                o = _dot(z, mg_ref[g])
                out_ref[:, kb, kah, :, g * FNET_GDIM:(g + 1) * FNET_GDIM] = o.reshape(R, T, FNET_GDIM)
            return carry

        lax.fori_loop(0, 2 * R, stage3, 0, unroll=FNET_UNROLL)


def _fnet(u5, w_fnet, tables, batch):
    m1, m2, m3, chan = tables
    R, T, W = RADIX, SUBLANES, FNET_HALF
    return pl.pallas_call(
        _fnet_kernel,
        grid=(batch,),
        in_specs=[pl.BlockSpec((None, R, R, R, FNET_WIDTH), lambda b: (b, 0, 0, 0, 0)),
                  pl.BlockSpec((FNET_GROUPS, FNET_GDIM, FNET_GDIM), lambda b: (0, 0, 0)),
                  pl.BlockSpec(m1.shape, lambda b: (0, 0)),
                  pl.BlockSpec(m2.shape, lambda b: (0, 0, 0)),
                  pl.BlockSpec(m3.shape, lambda b: (0, 0, 0, 0)),
                  pl.BlockSpec(chan.shape, lambda b: (0, 0))],
        out_specs=pl.BlockSpec((None, R, R, 2, T, FNET_WIDTH), lambda b: (b, 0, 0, 0, 0, 0)),
        out_shape=jax.ShapeDtypeStruct((batch, R, R, 2, T, FNET_WIDTH), F32),
        scratch_shapes=[pltpu.VMEM((2, R, R, T, W), F32), pltpu.VMEM((2, R, R, T, W), F32),
                        pltpu.VMEM((2, R, R, T, W), F32), pltpu.VMEM((2, R, R, T, W), F32),
                        pltpu.VMEM((FNET_GROUPS, 2 * FNET_GDIM, FNET_GDIM), BF16)],
        compiler_params=pltpu.CompilerParams(dimension_semantics=("arbitrary",),
                                             vmem_limit_bytes=VMEM_LIMIT_BYTES),
        name="fnet",
    )(u5, w_fnet, m1, m2, m3, chan)


def _out_kernel(x_ref, og_ref, zg_ref, uf_ref, zf_ref, w_ref, lng_ref, lnb_ref, o_ref):
    y_gla = (og_ref[...].astype(F32) * _silu(zg_ref[...].astype(F32))).astype(BF16)
    y_fnet = (uf_ref[...] * _silu(zf_ref[...].astype(F32))).astype(BF16)
    y = _dot(y_gla, w_ref[:GLA_WIDTH, :]) + _dot(y_fnet, w_ref[GLA_WIDTH:, :])
    r = DEEPNORM_ALPHA * x_ref[...] + y
    mu = jnp.mean(r, axis=-1, keepdims=True)
    d = r - mu
    var = jnp.mean(d * d, axis=-1, keepdims=True)
    o_ref[...] = d * lax.rsqrt(var + LN_EPS) * lng_ref[...] + lnb_ref[...]


def _out_proj(x2d, o_gla, zg, uf, zf, w_out, ln_g, ln_b):
    m = x2d.shape[0]
    row = lambda i: (i, 0)
    const = lambda i: (0, 0)
    return pl.pallas_call(
        _out_kernel,
        grid=(m // ROW_TILE,),
        in_specs=[pl.BlockSpec((ROW_TILE, D_MODEL), row),
                  pl.BlockSpec((ROW_TILE, GLA_WIDTH), row),
                  pl.BlockSpec((ROW_TILE, GLA_WIDTH), row),
                  pl.BlockSpec((ROW_TILE, FNET_WIDTH), row),
                  pl.BlockSpec((ROW_TILE, FNET_WIDTH), row),
                  pl.BlockSpec((GLA_WIDTH + FNET_WIDTH, D_MODEL), const),
                  pl.BlockSpec((1, D_MODEL), const),
                  pl.BlockSpec((1, D_MODEL), const)],
        out_specs=pl.BlockSpec((ROW_TILE, D_MODEL), row),
        out_shape=jax.ShapeDtypeStruct((m, D_MODEL), F32),
        compiler_params=pltpu.CompilerParams(dimension_semantics=("arbitrary",),
                                             vmem_limit_bytes=VMEM_LIMIT_BYTES),
        name="out_proj",
    )(x2d, o_gla, zg, uf, zf, w_out, ln_g, ln_b)


def _permute_w_in(w):
    kw, gw, fw, r = GLA_KEY_WIDTH, GLA_WIDTH, FNET_WIDTH, GATE_RANK
    o_q, o_k, o_v = 0, kw, 2 * kw
    o_gf = o_v + gw
    o_zg = o_gf + 2 * r
    o_u = o_zg + gw
    o_zf = o_u + fw
    gates = jnp.concatenate([w[:, o_gf:o_gf + 2 * r]] * (GATE_PAD // (2 * r)), axis=1)
    cols = [w[:, o_q:o_q + kw] * (GLA_DK ** -0.5), w[:, o_k:o_k + kw], w[:, o_v:o_v + gw], w[:, o_zg:o_zg + gw],
            w[:, o_u:o_u + fw], w[:, o_zf:o_zf + fw], gates]
    return jnp.concatenate(cols, axis=1).astype(BF16)


def kernel(x, w_in, w_gate_up_fwd, b_gate_fwd, w_gate_up_bwd, b_gate_bwd, gla_norm_g,
           w_fnet, w_out, ln_g, ln_b):
    batch, seq, d = x.shape
    assert (seq, d) == (SEQ, D_MODEL) and w_in.shape[0] == DEPTH
    tables = _fnet_tables()
    for l in range(DEPTH):
        x2d = x.reshape(batch * seq, d)
        q, k, v, zg, u, zf, g = _projection(x2d, _permute_w_in(w_in[l]))

        zero = jnp.zeros((GATE_RANK, GLA_KEY_WIDTH), F32)
        w_bd = jnp.block([[w_gate_up_fwd[l], zero], [zero, w_gate_up_bwd[l]]]) * LOG2_E
        w_hi = w_bd.astype(BF16)
        w_lo = (w_bd - w_hi.astype(F32)).astype(BF16)
        wg = jnp.concatenate([w_hi, w_hi, w_lo, jnp.zeros_like(w_hi)], axis=0)
        bg = jnp.concatenate([b_gate_fwd[l], b_gate_bwd[l]])[None, :] * LOG2_E
        o_gla = _gla(q, k, v, g, wg, bg, gla_norm_g[l][None, :], batch)

        u5 = u.reshape(batch, RADIX, RADIX, RADIX, FNET_WIDTH)
        uf = _fnet(u5, w_fnet[l], tables, batch).reshape(batch * seq, FNET_WIDTH)

        out = _out_proj(x2d, o_gla, zg, uf, zf, w_out[l].astype(BF16), ln_g[l][None, :],
                        ln_b[l][None, :])
        x = out.reshape(batch, seq, d)
    return x
```

```python
import functools
import math

import numpy as np
import jax
import jax.numpy as jnp
from jax import lax
from jax.experimental import pallas as pl
from jax.experimental.pallas import tpu as pltpu

F32 = jnp.float32
BF16 = jnp.bfloat16

D_MODEL = 1024
SEQ = 4096
GLA_HEADS = 4
GLA_DK = 64
GLA_DV = 128
GLA_KEY_WIDTH = GLA_HEADS * GLA_DK
GLA_WIDTH = GLA_HEADS * GLA_DV
GATE_RANK = 16
GATE_TAU = 16.0
FNET_GROUPS = 4
FNET_GDIM = 128
FNET_WIDTH = FNET_GROUPS * FNET_GDIM
LN_EPS = 1e-5
RMS_EPS = 1e-6
DEPTH = 1
DEEPNORM_ALPHA = (2.0 * DEPTH) ** 0.25
LOG2_E = 1.0 / math.log(2.0)

LANES = 128
SUBLANES = 8
VMEM_LIMIT_BYTES = 60 * 1024 * 1024

ROW_TILE = 512
OUT_ROW_TILE = 1024
OUT_SUBTILES = 4
GLA_CHUNK = 256
GATE_PAD = LANES
RADIX = 16
FNET_HALF = 256
GLA_UNROLL = 8
FNET_UNROLL = 32

_PROJ_COLS = (("q", GLA_KEY_WIDTH), ("k", GLA_KEY_WIDTH), ("v", GLA_WIDTH), ("zg", GLA_WIDTH),
              ("u", FNET_WIDTH), ("zf", FNET_WIDTH), ("g", GATE_PAD))
PROJ_WIDTH = sum(w for _, w in _PROJ_COLS)


def _dot(a, b):
    return jnp.dot(a, b, preferred_element_type=F32)


def _split2(x):
    hi = x.astype(BF16)
    lo = (x - hi.astype(F32)).astype(BF16)
    return hi, lo


def _split3(x):
    hi = x.astype(BF16)
    r = x - hi.astype(F32)
    mid = r.astype(BF16)
    lo = (r - mid.astype(F32)).astype(BF16)
    return hi, mid, lo


def _dot_f32(a, b):
    ah, al = _split2(a)
    bh, bl = _split2(b)
    return _dot(ah, bh) + _dot(ah, bl) + _dot(al, bh)


def _silu(z):
    h = 0.5 * z
    return h + h * jnp.tanh(h)


def _proj_kernel(x_ref, w_ref, q_ref, k_ref, v_ref, zg_ref, u_ref, zf_ref, g_ref):
    xb = x_ref[...].astype(BF16)
    outs = (q_ref, k_ref, v_ref, zg_ref, u_ref, zf_ref, g_ref)
    off = 0
    for (name, width), o_ref in zip(_PROJ_COLS, outs):
        r = _dot(xb, w_ref[:, off:off + width])
        if name == "g":
            hi, lo = _split2(r)
            lane = lax.broadcasted_iota(jnp.int32, r.shape, 1)
            r = jnp.where((lane >= 2 * GATE_RANK) & (lane < 4 * GATE_RANK), lo, hi)
        o_ref[...] = r.astype(o_ref.dtype)
        off += width


def _projection(x2d, w_perm):
    m = x2d.shape[0]
    grid = (m // ROW_TILE,)
    row = lambda i: (i, 0)
    out_shape = [jax.ShapeDtypeStruct((m, w), BF16) for _, w in _PROJ_COLS]
    out_specs = [pl.BlockSpec((ROW_TILE, w), row) for _, w in _PROJ_COLS]
    return pl.pallas_call(
        _proj_kernel,
        grid=grid,
        in_specs=[pl.BlockSpec((ROW_TILE, D_MODEL), row),
                  pl.BlockSpec((D_MODEL, PROJ_WIDTH), lambda i: (0, 0))],
        out_specs=out_specs,
        out_shape=out_shape,
        compiler_params=pltpu.CompilerParams(dimension_semantics=("arbitrary",),
                                             vmem_limit_bytes=VMEM_LIMIT_BYTES),
        name="in_proj",
    )(x2d, w_perm)


def _gla_kernel(q_ref, k_ref, v_ref, g_ref, wg_ref, bg_ref, gam_ref, y_ref,
                oacc_ref, qd_ref, ki_ref, kdt_ref, upd_ref, dec_ref, dect_ref):
    C = GLA_CHUNK
    KW = GLA_KEY_WIDTH
    n_chunks = SEQ // C
    row_i = lax.broadcasted_iota(jnp.int32, (C, C), 0)
    col_i = lax.broadcasted_iota(jnp.int32, (C, C), 1)
    lower = row_i >= col_i
    upper = row_i <= col_i
    lane_head = lax.broadcasted_iota(jnp.int32, (C, KW), 1) // GLA_DK
    state_head = lax.broadcasted_iota(jnp.int32, (KW, GLA_DV), 0) // GLA_DK
    pair_ones = (lax.broadcasted_iota(jnp.int32, (2 * GLA_DV, 2 * GLA_DV), 0) // GLA_DV ==
                 lax.broadcasted_iota(jnp.int32, (2 * GLA_DV, 2 * GLA_DV), 1) // GLA_DV).astype(BF16)

    def prep(n, carry):
        rows = pl.ds(pl.multiple_of(n * C, C), C)
        z2 = _dot(g_ref[rows, :], wg_ref[...]) + bg_ref[...]
        tau_log2_a = jnp.minimum(z2, 0.0) - jnp.log2(1.0 + jnp.exp2(-jnp.abs(z2)))
        hi = tau_log2_a.astype(BF16)
        mid = (tau_log2_a - hi.astype(F32)).astype(BF16)
        lo_t = jnp.where(lower, 1.0 / GATE_TAU, 0.0).astype(BF16)
        up_t = jnp.where(upper, 1.0 / GATE_TAU, 0.0).astype(BF16)
        b_f = _dot(lo_t, hi[:, :KW]) + _dot(lo_t, mid[:, :KW])
        b_b = _dot(up_t, hi[:, KW:]) + _dot(up_t, mid[:, KW:])
        e_f = b_f[C - 1:C, :]
        e_b = b_b[0:1, :]
        dec_ref[n, 0] = jnp.broadcast_to(jnp.exp2(e_f), (SUBLANES, KW))
        dec_ref[n, 1] = jnp.broadcast_to(jnp.exp2(e_b), (SUBLANES, KW))

        qf = q_ref[rows, :].astype(F32)
        kf = k_ref[rows, :].astype(F32)
        qd_ref[rows, :KW] = (qf * jnp.exp2(b_f)).astype(BF16)
        qd_ref[rows, KW:] = (qf * jnp.exp2(b_b)).astype(BF16)
        ki_ref[rows, :KW] = (kf * jnp.exp2(-b_f)).astype(BF16)
        ki_ref[rows, KW:] = (kf * jnp.exp2(-b_b)).astype(BF16)
        kd = jnp.concatenate([kf * jnp.exp2(e_f - b_f), kf * jnp.exp2(e_b - b_b)], axis=1)
        kdt_ref[n] = jnp.transpose(kd).astype(BF16)
        return carry

    lax.fori_loop(0, n_chunks, prep, 0, unroll=GLA_UNROLL)

    def attend(n, carry):
        rows = pl.ds(pl.multiple_of(n * C, C), C)
        nt = (((1,), (1,)), ((), ()))
        qd_f = qd_ref[rows, :KW]
        qd_b = qd_ref[rows, KW:]
        probs = []
        for h in range(GLA_HEADS):
            in_head = lane_head == h
            s_f = lax.dot_general(jnp.where(in_head, qd_f, jnp.zeros_like(qd_f)), ki_ref[rows, :KW],
                                  nt, preferred_element_type=F32)
            s_b = lax.dot_general(jnp.where(in_head, qd_b, jnp.zeros_like(qd_b)), ki_ref[rows, KW:],
                                  nt, preferred_element_type=F32)
            probs.append((jnp.where(lower, s_f, 0.0) + jnp.where(upper, s_b, 0.0)).astype(BF16))
        for h in range(GLA_HEADS):
            vcols = slice(h * GLA_DV, (h + 1) * GLA_DV)
            lhs = jnp.concatenate([probs[h], kdt_ref[n, h * GLA_DK:(h + 1) * GLA_DK, :],
                                   kdt_ref[n, KW + h * GLA_DK:KW + (h + 1) * GLA_DK, :]], axis=0)
            r = _dot(lhs, v_ref[rows, vcols])
            oacc_ref[rows, vcols] = r[:C]
            upd_ref[n, 0, h * GLA_DK:(h + 1) * GLA_DK, :] = r[C:C + GLA_DK]
            upd_ref[n, 1, h * GLA_DK:(h + 1) * GLA_DK, :] = r[C + GLA_DK:]
        for d in range(2):
            dect_ref[n, d] = jnp.transpose(jnp.broadcast_to(dec_ref[n, d][0:1, :], (LANES, KW)))
        return carry

    lax.fori_loop(0, n_chunks, attend, 0, unroll=GLA_UNROLL)

    def expand(state):
        sb = state.astype(BF16)
        return jnp.concatenate([jnp.where(state_head == h, sb, jnp.zeros_like(sb))
                                for h in range(GLA_HEADS)], axis=1)

    def advance(n, d, state, qcols):
        rows = pl.ds(pl.multiple_of(n * C, C), C)
        oacc_ref[rows, :] += _dot(qd_ref[rows, qcols], expand(state))
        return state * dect_ref[n, d] + upd_ref[n, d]

    def finalize(n):
        rows = pl.ds(pl.multiple_of(n * C, C), C)
        for pair in range(GLA_HEADS // 2):
            cols = slice(2 * pair * GLA_DV, 2 * (pair + 1) * GLA_DV)
            o_p = oacc_ref[rows, cols]
            ms = _dot((o_p * o_p).astype(BF16), pair_ones) * (1.0 / GLA_DV)
            y_ref[rows, cols] = (o_p * lax.rsqrt(ms + RMS_EPS) * gam_ref[:, cols]).astype(y_ref.dtype)

    def inter(j, states, done):
        s_f, s_b = states
        m = n_chunks - 1 - j
        s_f = advance(j, 0, s_f, slice(0, KW))
        s_b = advance(m, 1, s_b, slice(KW, 2 * KW))
        if done:
            finalize(j)
            finalize(m)
        return s_f, s_b

    state0 = jnp.zeros((KW, GLA_DV), F32)
    states = lax.fori_loop(0, n_chunks // 2, lambda j, st: inter(j, st, False), (state0, state0),
                           unroll=GLA_UNROLL)
    lax.fori_loop(n_chunks // 2, n_chunks, lambda j, st: inter(j, st, True), states,
                  unroll=GLA_UNROLL)


def _gla(q, k, v, g, wg, bg, gamma, batch):
    tok = lambda b: (b, 0)
    const = lambda b: (0, 0)
    n_chunks = SEQ // GLA_CHUNK
    return pl.pallas_call(
        _gla_kernel,
        grid=(batch,),
        in_specs=[pl.BlockSpec((SEQ, GLA_KEY_WIDTH), tok),
                  pl.BlockSpec((SEQ, GLA_KEY_WIDTH), tok),
                  pl.BlockSpec((SEQ, GLA_WIDTH), tok),
                  pl.BlockSpec((SEQ, GATE_PAD), tok),
                  pl.BlockSpec((GATE_PAD, 2 * GLA_KEY_WIDTH), const),
                  pl.BlockSpec((1, 2 * GLA_KEY_WIDTH), const),
                  pl.BlockSpec((1, GLA_WIDTH), const)],
        out_specs=pl.BlockSpec((SEQ, GLA_WIDTH), tok),
        out_shape=jax.ShapeDtypeStruct((batch * SEQ, GLA_WIDTH), BF16),
        scratch_shapes=[pltpu.VMEM((SEQ, GLA_WIDTH), F32),
                        pltpu.VMEM((SEQ, 2 * GLA_KEY_WIDTH), BF16),
                        pltpu.VMEM((SEQ, 2 * GLA_KEY_WIDTH), BF16),
                        pltpu.VMEM((n_chunks, 2 * GLA_KEY_WIDTH, GLA_CHUNK), BF16),
                        pltpu.VMEM((n_chunks, 2, GLA_KEY_WIDTH, GLA_DV), F32),
                        pltpu.VMEM((n_chunks, 2, SUBLANES, GLA_KEY_WIDTH), F32),
                        pltpu.VMEM((n_chunks, 2, GLA_KEY_WIDTH, LANES), F32)],
        compiler_params=pltpu.CompilerParams(dimension_semantics=("arbitrary",),
                                             vmem_limit_bytes=VMEM_LIMIT_BYTES),
        name="gla",
    )(q, k, v, g, wg, bg, gamma)


def _fnet_tables():
    R = RADIX
    n = SEQ
    a = np.arange(R)
    lo8 = np.arange(SUBLANES)
    eye8 = np.eye(SUBLANES)

    ka = (8 * np.arange(2)[:, None] + lo8[None, :])
    ang = 2 * np.pi * (256 * a[None, None, :] * ka[:, :, None]) / n
    eye16 = np.eye(R)
    def s1(trig):
        t = trig(ang)
        m = t[:, None, :, :, None] * eye16[None, :, None, None, :]
        return m.reshape(2 * R * SUBLANES, R * R)
    m1 = np.concatenate([s1(np.cos), -s1(np.sin)], axis=0)

    def complex_block(theta):
        c = np.cos(theta)
        s = -np.sin(theta)
        def kron(t):
            m = t[:, :, :, None] * eye8[None, :, None, :]
            return m.reshape(t.shape[0] * SUBLANES, t.shape[2] * SUBLANES)
        mr, mi = kron(c), kron(s)
        return np.block([[mr, -mi], [mi, mr]])

    m2 = []
    for kah in range(2):
        k_a = 8 * kah + lo8
        expo = 16 * a[None, None, :] * k_a[None, :, None] + 256 * a[None, None, :] * a[:, None, None]
        m2.append(complex_block(2 * np.pi * expo / n))
    m2 = np.stack(m2)

    m3 = np.zeros((2, R, 2 * R * SUBLANES, 2 * R * SUBLANES))
    for kah in range(2):
        k_a = 8 * kah + lo8
        for kb in range(R):
            expo = (a[None, None, :] * k_a[None, :, None] + 16 * a[None, None, :] * kb
                    + 256 * a[None, None, :] * a[:, None, None])
            m3[kah, kb] = complex_block(2 * np.pi * expo / n)

    c = np.arange(FNET_GDIM)
    ang_c = 2 * np.pi * np.outer(c, c) / FNET_GDIM
    norm = 1.0 / math.sqrt(SEQ * FNET_GDIM)
    chan = np.concatenate([np.cos(ang_c), np.sin(ang_c)], axis=0) * norm
    return (jnp.asarray(m1, F32).astype(BF16), jnp.asarray(m2, F32).astype(BF16),
            jnp.asarray(m3, F32).astype(BF16), jnp.asarray(chan, F32))


def _fnet_kernel(u_ref, zf_ref, wf_ref, m1_ref, m2_ref, m3_ref, chan_ref, out_ref,
                 y1r, y1i, y2r, y2i, mg_ref):
    R = RADIX
    T = SUBLANES
    W = FNET_HALF

    @pl.when(pl.program_id(0) == 0)
    def _():
        for g in range(FNET_GROUPS):
            mg_ref[g] = _dot_f32(chan_ref[...], wf_ref[g]).astype(BF16)

    for half in range(FNET_WIDTH // W):
        lanes = slice(half * W, (half + 1) * W)

        def stage1(sm, carry):
            blk = u_ref[:, sm, :, lanes].reshape(R * R, W)
            res = _dot(m1_ref[...], blk)
            y1r[:, :, sm, :, :] = res[:R * R].reshape(2, R, T, W)
            y1i[:, :, sm, :, :] = res[R * R:].reshape(2, R, T, W)
            return carry

        lax.fori_loop(0, R, stage1, 0, unroll=FNET_UNROLL)

        def stage2(i, carry):
            kah = i // R
            sl = i % R
            rhs = jnp.concatenate([y1r[kah, sl].reshape(R * T, W), y1i[kah, sl].reshape(R * T, W)],
                                  axis=0).astype(BF16)
            res = _dot(m2_ref[kah], rhs)
            y2r[kah, :, sl, :, :] = res[:R * T].reshape(R, T, W)
            y2i[kah, :, sl, :, :] = res[R * T:].reshape(R, T, W)
            return carry

        lax.fori_loop(0, 2 * R, stage2, 0, unroll=FNET_UNROLL)

        def stage3(kb, carry):
            mixed = []
            for kah in range(2):
                rhs = jnp.concatenate([y2r[kah, kb].reshape(R * T, W), y2i[kah, kb].reshape(R * T, W)],
                                      axis=0).astype(BF16)
                res = _dot(m3_ref[kah, kb], rhs)
                groups = []
                for j in range(W // FNET_GDIM):
                    gl = slice(j * FNET_GDIM, (j + 1) * FNET_GDIM)
                    z = jnp.concatenate([res[:R * T, gl], res[R * T:, gl]], axis=1).astype(BF16)
                    groups.append(_dot(z, mg_ref[half * (W // FNET_GDIM) + j]))
                mixed.append(jnp.concatenate(groups, axis=1).reshape(R, T, W))
            uf = jnp.stack(mixed, axis=1).reshape(R, 2 * T, W)
            gate = _silu(zf_ref[:, kb, :, lanes].astype(F32))
            out_ref[:, kb, :, lanes] = (uf * gate).astype(out_ref.dtype)
            return carry

        lax.fori_loop(0, R, stage3, 0, unroll=FNET_UNROLL)


def _fnet(u5, zf5, w_fnet, tables, batch):
    m1, m2, m3, chan = tables
    R, T, W = RADIX, SUBLANES, FNET_HALF
    return pl.pallas_call(
        _fnet_kernel,
        grid=(batch,),
        in_specs=[pl.BlockSpec((None, R, R, R, FNET_WIDTH), lambda b: (b, 0, 0, 0, 0)),
                  pl.BlockSpec((None, R, R, R, FNET_WIDTH), lambda b: (b, 0, 0, 0, 0)),
                  pl.BlockSpec((FNET_GROUPS, FNET_GDIM, FNET_GDIM), lambda b: (0, 0, 0)),
                  pl.BlockSpec(m1.shape, lambda b: (0, 0)),
                  pl.BlockSpec(m2.shape, lambda b: (0, 0, 0)),
                  pl.BlockSpec(m3.shape, lambda b: (0, 0, 0, 0)),
                  pl.BlockSpec(chan.shape, lambda b: (0, 0))],
        out_specs=pl.BlockSpec((None, R, R, R, FNET_WIDTH), lambda b: (b, 0, 0, 0, 0)),
        out_shape=jax.ShapeDtypeStruct((batch, R, R, R, FNET_WIDTH), BF16),
        scratch_shapes=[pltpu.VMEM((2, R, R, T, W), F32), pltpu.VMEM((2, R, R, T, W), F32),
                        pltpu.VMEM((2, R, R, T, W), F32), pltpu.VMEM((2, R, R, T, W), F32),
                        pltpu.VMEM((FNET_GROUPS, 2 * FNET_GDIM, FNET_GDIM), BF16)],
        compiler_params=pltpu.CompilerParams(dimension_semantics=("arbitrary",),
                                             vmem_limit_bytes=VMEM_LIMIT_BYTES),
        name="fnet",
    )(u5, zf5, w_fnet, m1, m2, m3, chan)


def _out_kernel(x_ref, og_ref, zg_ref, yf_ref, w_ref, lng_ref, lnb_ref, o_ref):
    sub = OUT_ROW_TILE // OUT_SUBTILES
    ys = []
    for t in range(OUT_SUBTILES):
        rows = slice(t * sub, (t + 1) * sub)
        y_gla = (og_ref[rows, :].astype(F32) * _silu(zg_ref[rows, :].astype(F32))).astype(BF16)
        ys.append(_dot(jnp.concatenate([y_gla, yf_ref[rows, :]], axis=1), w_ref[...]))
    for t in range(OUT_SUBTILES):
        rows = slice(t * sub, (t + 1) * sub)
        r = DEEPNORM_ALPHA * x_ref[rows, :] + ys[t]
        mu = jnp.mean(r, axis=-1, keepdims=True)
        d = r - mu
        var = jnp.mean(d * d, axis=-1, keepdims=True)
        o_ref[rows, :] = d * lax.rsqrt(var + LN_EPS) * lng_ref[...] + lnb_ref[...]


def _out_proj(x2d, o_gla, zg, y_fnet, w_out, ln_g, ln_b):
    m = x2d.shape[0]
    row = lambda i: (i, 0)
    const = lambda i: (0, 0)
    return pl.pallas_call(
        _out_kernel,
        grid=(m // OUT_ROW_TILE,),
        in_specs=[pl.BlockSpec((OUT_ROW_TILE, D_MODEL), row),
                  pl.BlockSpec((OUT_ROW_TILE, GLA_WIDTH), row),
                  pl.BlockSpec((OUT_ROW_TILE, GLA_WIDTH), row),
                  pl.BlockSpec((OUT_ROW_TILE, FNET_WIDTH), row),
                  pl.BlockSpec((GLA_WIDTH + FNET_WIDTH, D_MODEL), const),
                  pl.BlockSpec((1, D_MODEL), const),
                  pl.BlockSpec((1, D_MODEL), const)],
        out_specs=pl.BlockSpec((OUT_ROW_TILE, D_MODEL), row),
        out_shape=jax.ShapeDtypeStruct((m, D_MODEL), F32),
        compiler_params=pltpu.CompilerParams(dimension_semantics=("arbitrary",),
                                             vmem_limit_bytes=VMEM_LIMIT_BYTES),
        name="out_proj",
    )(x2d, o_gla, zg, y_fnet, w_out, ln_g, ln_b)


def _permute_w_in(w):
    kw, gw, fw, r = GLA_KEY_WIDTH, GLA_WIDTH, FNET_WIDTH, GATE_RANK
    o_q, o_k, o_v = 0, kw, 2 * kw
    o_gf = o_v + gw
    o_zg = o_gf + 2 * r
    o_u = o_zg + gw
    o_zf = o_u + fw
    gates = jnp.concatenate([w[:, o_gf:o_gf + 2 * r]] * (GATE_PAD // (2 * r)), axis=1)
    cols = [w[:, o_q:o_q + kw] * (GLA_DK ** -0.5), w[:, o_k:o_k + kw], w[:, o_v:o_v + gw], w[:, o_zg:o_zg + gw],
            w[:, o_u:o_u + fw], w[:, o_zf:o_zf + fw], gates]
    return jnp.concatenate(cols, axis=1).astype(BF16)


def kernel(x, w_in, w_gate_up_fwd, b_gate_fwd, w_gate_up_bwd, b_gate_bwd, gla_norm_g,
           w_fnet, w_out, ln_g, ln_b):
    batch, seq, d = x.shape
    assert (seq, d) == (SEQ, D_MODEL) and w_in.shape[0] == DEPTH
    tables = _fnet_tables()
    for l in range(DEPTH):
        x2d = x.reshape(batch * seq, d)
        q, k, v, zg, u, zf, g = _projection(x2d, _permute_w_in(w_in[l]))

        zero = jnp.zeros((GATE_RANK, GLA_KEY_WIDTH), F32)
        w_bd = jnp.block([[w_gate_up_fwd[l], zero], [zero, w_gate_up_bwd[l]]]) * LOG2_E
        w_hi = w_bd.astype(BF16)
        w_lo = (w_bd - w_hi.astype(F32)).astype(BF16)
        wg = jnp.concatenate([w_hi, w_hi, w_lo, jnp.zeros_like(w_hi)], axis=0)
        bg = jnp.concatenate([b_gate_fwd[l], b_gate_bwd[l]])[None, :] * LOG2_E
        o_gla = _gla(q, k, v, g, wg, bg, gla_norm_g[l][None, :], batch)

        seq_digits = (batch, RADIX, RADIX, RADIX, FNET_WIDTH)
        y_fnet = _fnet(u.reshape(seq_digits), zf.reshape(seq_digits), w_fnet[l], tables, batch)

        out = _out_proj(x2d, o_gla, zg, y_fnet.reshape(batch * seq, FNET_WIDTH),
                        w_out[l].astype(BF16), ln_g[l][None, :], ln_b[l][None, :])
        x = out.reshape(batch, seq, d)
    return x
```

```python
import functools
import math

import numpy as np
import jax
import jax.numpy as jnp
from jax import lax
from jax.experimental import pallas as pl
from jax.experimental.pallas import tpu as pltpu

F32 = jnp.float32
BF16 = jnp.bfloat16

D_MODEL = 1024
SEQ = 4096
GLA_HEADS = 4
GLA_DK = 64
GLA_DV = 128
GLA_KEY_WIDTH = GLA_HEADS * GLA_DK
GLA_WIDTH = GLA_HEADS * GLA_DV
GATE_RANK = 16
GATE_TAU = 16.0
FNET_GROUPS = 4
FNET_GDIM = 128
FNET_WIDTH = FNET_GROUPS * FNET_GDIM
LN_EPS = 1e-5
RMS_EPS = 1e-6
DEPTH = 1
DEEPNORM_ALPHA = (2.0 * DEPTH) ** 0.25
LOG2_E = 1.0 / math.log(2.0)

LANES = 128
SUBLANES = 8
VMEM_LIMIT_BYTES = 60 * 1024 * 1024

ROW_TILE = 1024
OUT_ROW_TILE = 1024
OUT_SUBTILES = 4
GLA_CHUNK = 256
GATE_PAD = LANES
RADIX = 16
FNET_HALF = 256
GLA_SAFE_LOG2_DECAY = 100.0
GLA_UNROLL = 8
FNET_UNROLL = 32

_PROJ_COLS = (("q", GLA_KEY_WIDTH), ("k", GLA_KEY_WIDTH), ("v", GLA_WIDTH), ("zg", GLA_WIDTH),
              ("u", FNET_WIDTH), ("zf", FNET_WIDTH), ("g", GATE_PAD))
PROJ_WIDTH = sum(w for _, w in _PROJ_COLS)


def _dot(a, b):
    return jnp.dot(a, b, preferred_element_type=F32)


def _split2(x):
    hi = x.astype(BF16)
    lo = (x - hi.astype(F32)).astype(BF16)
    return hi, lo


def _split3(x):
    hi = x.astype(BF16)
    r = x - hi.astype(F32)
    mid = r.astype(BF16)
    lo = (r - mid.astype(F32)).astype(BF16)
    return hi, mid, lo


def _dot_f32(a, b):
    ah, al = _split2(a)
    bh, bl = _split2(b)
    return _dot(ah, bh) + _dot(ah, bl) + _dot(al, bh)


def _silu(z):
    h = 0.5 * z
    return h + h * jnp.tanh(h)


def _proj_kernel(x_ref, w_ref, q_ref, k_ref, v_ref, zg_ref, u_ref, zf_ref, g_ref):
    xb = x_ref[...].astype(BF16)
    outs = (q_ref, k_ref, v_ref, zg_ref, u_ref, zf_ref, g_ref)
    off = 0
    for (name, width), o_ref in zip(_PROJ_COLS, outs):
        r = _dot(xb, w_ref[:, off:off + width])
        if name == "g":
            hi, lo = _split2(r)
            lane = lax.broadcasted_iota(jnp.int32, r.shape, 1)
            r = jnp.where((lane >= 2 * GATE_RANK) & (lane < 4 * GATE_RANK), lo, hi)
        o_ref[...] = r.astype(o_ref.dtype)
        off += width


def _projection(x2d, w_perm):
    m = x2d.shape[0]
    grid = (m // ROW_TILE,)
    row = lambda i: (i, 0)
    out_shape = [jax.ShapeDtypeStruct((m, w), BF16) for _, w in _PROJ_COLS]
    out_specs = [pl.BlockSpec((ROW_TILE, w), row) for _, w in _PROJ_COLS]
    return pl.pallas_call(
        _proj_kernel,
        grid=grid,
        in_specs=[pl.BlockSpec((ROW_TILE, D_MODEL), row),
                  pl.BlockSpec((D_MODEL, PROJ_WIDTH), lambda i: (0, 0))],
        out_specs=out_specs,
        out_shape=out_shape,
        compiler_params=pltpu.CompilerParams(dimension_semantics=("arbitrary",),
                                             vmem_limit_bytes=VMEM_LIMIT_BYTES),
        name="in_proj",
    )(x2d, w_perm)


def _gla_kernel(q_ref, k_ref, v_ref, g_ref, wg_ref, bg_ref, gam_ref, y_ref,
                oacc_ref, qd_ref, ki_ref, kdt_ref, upd_ref, dec_ref, dect_ref,
                flag_ref, bsc_ref, qsc_ref, psc_ref):
    C = GLA_CHUNK
    KW = GLA_KEY_WIDTH
    n_chunks = SEQ // C
    row_i = lax.broadcasted_iota(jnp.int32, (C, C), 0)
    col_i = lax.broadcasted_iota(jnp.int32, (C, C), 1)
    lower = row_i >= col_i
    upper = row_i <= col_i
    lane_head = lax.broadcasted_iota(jnp.int32, (C, KW), 1) // GLA_DK
    state_head = lax.broadcasted_iota(jnp.int32, (KW, GLA_DV), 0) // GLA_DK
    pair_ones = (lax.broadcasted_iota(jnp.int32, (2 * GLA_DV, 2 * GLA_DV), 0) // GLA_DV ==
                 lax.broadcasted_iota(jnp.int32, (2 * GLA_DV, 2 * GLA_DV), 1) // GLA_DV).astype(BF16)

    def cum_log_decays(rows):
        z2 = _dot(g_ref[rows, :], wg_ref[...]) + bg_ref[...]
        tau_log2_a = jnp.minimum(z2, 0.0) - jnp.log2(1.0 + jnp.exp2(-jnp.abs(z2)))
        hi = tau_log2_a.astype(BF16)
        mid = (tau_log2_a - hi.astype(F32)).astype(BF16)
        lo_t = jnp.where(lower, 1.0 / GATE_TAU, 0.0).astype(BF16)
        up_t = jnp.where(upper, 1.0 / GATE_TAU, 0.0).astype(BF16)
        b_f = _dot(lo_t, hi[:, :KW]) + _dot(lo_t, mid[:, :KW])
        b_b = _dot(up_t, hi[:, KW:]) + _dot(up_t, mid[:, KW:])
        return b_f, b_b

    def prep(n, carry):
        rows = pl.ds(pl.multiple_of(n * C, C), C)
        b_f, b_b = cum_log_decays(rows)
        e_f = b_f[C - 1:C, :]
        e_b = b_b[0:1, :]
        dec_ref[n, 0] = jnp.broadcast_to(jnp.exp2(e_f), (SUBLANES, KW))
        dec_ref[n, 1] = jnp.broadcast_to(jnp.exp2(e_b), (SUBLANES, KW))
        flag_ref[n] = (jnp.max(-jnp.minimum(e_f, e_b)) > GLA_SAFE_LOG2_DECAY).astype(jnp.int32)

        qf = q_ref[rows, :].astype(F32)
        kf = k_ref[rows, :].astype(F32)
        qd_ref[rows, :KW] = (qf * jnp.exp2(b_f)).astype(BF16)
        qd_ref[rows, KW:] = (qf * jnp.exp2(b_b)).astype(BF16)
        ki_ref[rows, :KW] = (kf * jnp.exp2(-b_f)).astype(BF16)
        ki_ref[rows, KW:] = (kf * jnp.exp2(-b_b)).astype(BF16)
        kd = jnp.concatenate([kf * jnp.exp2(e_f - b_f), kf * jnp.exp2(e_b - b_b)], axis=1)
        kdt_ref[n] = jnp.transpose(kd).astype(BF16)
        return carry

    lax.fori_loop(0, n_chunks, prep, 0, unroll=GLA_UNROLL)

    def attend(n, carry):
        rows = pl.ds(pl.multiple_of(n * C, C), C)
        nt = (((1,), (1,)), ((), ()))
        qd_f = qd_ref[rows, :KW]
        qd_b = qd_ref[rows, KW:]
        probs = []
        for h in range(GLA_HEADS):
            in_head = lane_head == h
            s_f = lax.dot_general(jnp.where(in_head, qd_f, jnp.zeros_like(qd_f)), ki_ref[rows, :KW],
                                  nt, preferred_element_type=F32)
            s_b = lax.dot_general(jnp.where(in_head, qd_b, jnp.zeros_like(qd_b)), ki_ref[rows, KW:],
                                  nt, preferred_element_type=F32)
            probs.append((jnp.where(lower, s_f, 0.0) + jnp.where(upper, s_b, 0.0)).astype(BF16))
        for h in range(GLA_HEADS):
            vcols = slice(h * GLA_DV, (h + 1) * GLA_DV)
            lhs = jnp.concatenate([probs[h], kdt_ref[n, h * GLA_DK:(h + 1) * GLA_DK, :],
                                   kdt_ref[n, KW + h * GLA_DK:KW + (h + 1) * GLA_DK, :]], axis=0)
            r = _dot(lhs, v_ref[rows, vcols])
            oacc_ref[rows, vcols] = r[:C]
            upd_ref[n, 0, h * GLA_DK:(h + 1) * GLA_DK, :] = r[C:C + GLA_DK]
            upd_ref[n, 1, h * GLA_DK:(h + 1) * GLA_DK, :] = r[C + GLA_DK:]
        for d in range(2):
            dect_ref[n, d] = jnp.transpose(jnp.broadcast_to(dec_ref[n, d][0:1, :], (LANES, KW)))
        return carry

    lax.fori_loop(0, n_chunks, attend, 0, unroll=GLA_UNROLL)

    def redo_exact(n, carry):
        @pl.when(flag_ref[n] != 0)
        def _():
            rows = pl.ds(pl.multiple_of(n * C, C), C)
            nt = (((1,), (1,)), ((), ()))
            b_f, b_b = cum_log_decays(rows)
            bsc_ref[:, :KW] = b_f
            bsc_ref[:, KW:] = b_b
            qsc_ref[...] = q_ref[rows, :].astype(F32)
            col = lax.broadcasted_iota(jnp.int32, (SUBLANES, C), 1)
            head_row = (lax.broadcasted_iota(jnp.int32, (SUBLANES, KW), 0) ==
                        lax.broadcasted_iota(jnp.int32, (SUBLANES, KW), 1) // GLA_DK)

            def row_scores(i, c):
                kf = k_ref[rows, :].astype(F32)
                q_heads = jnp.where(head_row, jnp.broadcast_to(qsc_ref[pl.ds(i, 1), :], (SUBLANES, KW)),
                                    0.0).astype(BF16)
                w_f = jnp.exp2(jnp.minimum(bsc_ref[pl.ds(i, 1), :KW] - bsc_ref[:, :KW], 0.0))
                w_b = jnp.exp2(jnp.minimum(bsc_ref[pl.ds(i, 1), KW:] - bsc_ref[:, KW:], 0.0))
                s_f = lax.dot_general(q_heads, (kf * w_f).astype(BF16), nt, preferred_element_type=F32)
                s_b = lax.dot_general(q_heads, (kf * w_b).astype(BF16), nt, preferred_element_type=F32)
                p = jnp.where(col <= i, s_f, 0.0) + jnp.where(col >= i, s_b, 0.0)
                for h in range(GLA_HEADS):
                    psc_ref[h, pl.ds(i, 1), :] = p[h:h + 1, :]
                return c

            lax.fori_loop(0, C, row_scores, 0)
            for h in range(GLA_HEADS):
                vcols = slice(h * GLA_DV, (h + 1) * GLA_DV)
                oacc_ref[rows, vcols] = _dot(psc_ref[h].astype(BF16), v_ref[rows, vcols])
        return carry

    lax.fori_loop(0, n_chunks, redo_exact, 0)

    def expand(state):
        sb = state.astype(BF16)
        return jnp.concatenate([jnp.where(state_head == h, sb, jnp.zeros_like(sb))
                                for h in range(GLA_HEADS)], axis=1)

    def advance(n, d, state, qcols):
        rows = pl.ds(pl.multiple_of(n * C, C), C)
        oacc_ref[rows, :] += _dot(qd_ref[rows, qcols], expand(state))
        return state * dect_ref[n, d] + upd_ref[n, d]

    def finalize(n):
        rows = pl.ds(pl.multiple_of(n * C, C), C)
        for pair in range(GLA_HEADS // 2):
            cols = slice(2 * pair * GLA_DV, 2 * (pair + 1) * GLA_DV)
            o_p = oacc_ref[rows, cols]
            ms = _dot((o_p * o_p).astype(BF16), pair_ones) * (1.0 / GLA_DV)
            y_ref[rows, cols] = (o_p * lax.rsqrt(ms + RMS_EPS) * gam_ref[:, cols]).astype(y_ref.dtype)

    def inter(j, states, done):
        s_f, s_b = states
        m = n_chunks - 1 - j
        s_f = advance(j, 0, s_f, slice(0, KW))
        s_b = advance(m, 1, s_b, slice(KW, 2 * KW))
        if done:
            finalize(j)
            finalize(m)
        return s_f, s_b

    state0 = jnp.zeros((KW, GLA_DV), F32)
    states = lax.fori_loop(0, n_chunks // 2, lambda j, st: inter(j, st, False), (state0, state0),
                           unroll=GLA_UNROLL)
    lax.fori_loop(n_chunks // 2, n_chunks, lambda j, st: inter(j, st, True), states,
                  unroll=GLA_UNROLL)


def _gla(q, k, v, g, wg, bg, gamma, batch):
    tok = lambda b: (b, 0)
    const = lambda b: (0, 0)
    n_chunks = SEQ // GLA_CHUNK
    return pl.pallas_call(
        _gla_kernel,
        grid=(batch,),
        in_specs=[pl.BlockSpec((SEQ, GLA_KEY_WIDTH), tok),
                  pl.BlockSpec((SEQ, GLA_KEY_WIDTH), tok),
                  pl.BlockSpec((SEQ, GLA_WIDTH), tok),
                  pl.BlockSpec((SEQ, GATE_PAD), tok),
                  pl.BlockSpec((GATE_PAD, 2 * GLA_KEY_WIDTH), const),
                  pl.BlockSpec((1, 2 * GLA_KEY_WIDTH), const),
                  pl.BlockSpec((1, GLA_WIDTH), const)],
        out_specs=pl.BlockSpec((SEQ, GLA_WIDTH), tok),
        out_shape=jax.ShapeDtypeStruct((batch * SEQ, GLA_WIDTH), BF16),
        scratch_shapes=[pltpu.VMEM((SEQ, GLA_WIDTH), F32),
                        pltpu.VMEM((SEQ, 2 * GLA_KEY_WIDTH), BF16),
                        pltpu.VMEM((SEQ, 2 * GLA_KEY_WIDTH), BF16),
                        pltpu.VMEM((n_chunks, 2 * GLA_KEY_WIDTH, GLA_CHUNK), BF16),
                        pltpu.VMEM((n_chunks, 2, GLA_KEY_WIDTH, GLA_DV), F32),
                        pltpu.VMEM((n_chunks, 2, SUBLANES, GLA_KEY_WIDTH), F32),
                        pltpu.VMEM((n_chunks, 2, GLA_KEY_WIDTH, LANES), F32),
                        pltpu.SMEM((n_chunks,), jnp.int32),
                        pltpu.VMEM((GLA_CHUNK, 2 * GLA_KEY_WIDTH), F32),
                        pltpu.VMEM((GLA_CHUNK, GLA_KEY_WIDTH), F32),
                        pltpu.VMEM((GLA_HEADS, GLA_CHUNK, GLA_CHUNK), F32)],
        compiler_params=pltpu.CompilerParams(dimension_semantics=("arbitrary",),
                                             vmem_limit_bytes=VMEM_LIMIT_BYTES),
        name="gla",
    )(q, k, v, g, wg, bg, gamma)


def _fnet_tables():
    R = RADIX
    n = SEQ
    a = np.arange(R)
    lo8 = np.arange(SUBLANES)
    eye8 = np.eye(SUBLANES)

    ka = (8 * np.arange(2)[:, None] + lo8[None, :])
    ang = 2 * np.pi * (256 * a[None, None, :] * ka[:, :, None]) / n
    eye16 = np.eye(R)
    def s1(trig):
        t = trig(ang)
        m = t[:, None, :, :, None] * eye16[None, :, None, None, :]
        return m.reshape(2 * R * SUBLANES, R * R)
    m1 = np.concatenate([s1(np.cos), -s1(np.sin)], axis=0)

    def complex_block(theta):
        c = np.cos(theta)
        s = -np.sin(theta)
        def kron(t):
            m = t[:, :, :, None] * eye8[None, :, None, :]
            return m.reshape(t.shape[0] * SUBLANES, t.shape[2] * SUBLANES)
        mr, mi = kron(c), kron(s)
        return np.block([[mr, -mi], [mi, mr]])

    m2 = []
    for kah in range(2):
        k_a = 8 * kah + lo8
        expo = 16 * a[None, None, :] * k_a[None, :, None] + 256 * a[None, None, :] * a[:, None, None]
        m2.append(complex_block(2 * np.pi * expo / n))
    m2 = np.stack(m2)

    m3 = np.zeros((2, R, 2 * R * SUBLANES, 2 * R * SUBLANES))
    for kah in range(2):
        k_a = 8 * kah + lo8
        for kb in range(R):
            expo = (a[None, None, :] * k_a[None, :, None] + 16 * a[None, None, :] * kb
                    + 256 * a[None, None, :] * a[:, None, None])
            m3[kah, kb] = complex_block(2 * np.pi * expo / n)

    c = np.arange(FNET_GDIM)
    ang_c = 2 * np.pi * np.outer(c, c) / FNET_GDIM
    norm = 1.0 / math.sqrt(SEQ * FNET_GDIM)
    chan = np.concatenate([np.cos(ang_c), np.sin(ang_c)], axis=0) * norm
    return (jnp.asarray(m1, F32).astype(BF16), jnp.asarray(m2, F32).astype(BF16),
            jnp.asarray(m3, F32).astype(BF16), jnp.asarray(chan, F32))


def _fnet_kernel(u_ref, zf_ref, wf_ref, m1_ref, m2_ref, m3_ref, chan_ref, out_ref,
                 y1r, y1i, y2r, y2i, mg_ref):
    R = RADIX
    T = SUBLANES
    W = FNET_HALF

    @pl.when(pl.program_id(0) == 0)
    def _():
        for g in range(FNET_GROUPS):
            mg_ref[g] = _dot_f32(chan_ref[...], wf_ref[g]).astype(BF16)

    for half in range(FNET_WIDTH // W):
        lanes = slice(half * W, (half + 1) * W)

        def stage1(sm, carry):
            blk = u_ref[:, sm, :, lanes].reshape(R * R, W)
            res = _dot(m1_ref[...], blk)
            y1r[:, :, sm, :, :] = res[:R * R].reshape(2, R, T, W)
            y1i[:, :, sm, :, :] = res[R * R:].reshape(2, R, T, W)
            return carry

        lax.fori_loop(0, R, stage1, 0, unroll=FNET_UNROLL)

        def stage2(i, carry):
            kah = i // R
            sl = i % R
            rhs = jnp.concatenate([y1r[kah, sl].reshape(R * T, W), y1i[kah, sl].reshape(R * T, W)],
                                  axis=0).astype(BF16)
            res = _dot(m2_ref[kah], rhs)
            y2r[kah, :, sl, :, :] = res[:R * T].reshape(R, T, W)
            y2i[kah, :, sl, :, :] = res[R * T:].reshape(R, T, W)
            return carry

        lax.fori_loop(0, 2 * R, stage2, 0, unroll=FNET_UNROLL)

        def stage3(kb, carry):
            mixed = []
            for kah in range(2):
                rhs = jnp.concatenate([y2r[kah, kb].reshape(R * T, W), y2i[kah, kb].reshape(R * T, W)],
                                      axis=0).astype(BF16)
                res = _dot(m3_ref[kah, kb], rhs)
                groups = []
                for j in range(W // FNET_GDIM):
                    gl = slice(j * FNET_GDIM, (j + 1) * FNET_GDIM)
                    z = jnp.concatenate([res[:R * T, gl], res[R * T:, gl]], axis=1).astype(BF16)
                    groups.append(_dot(z, mg_ref[half * (W // FNET_GDIM) + j]))
                mixed.append(jnp.concatenate(groups, axis=1).reshape(R, T, W))
            uf = jnp.stack(mixed, axis=1).reshape(R, 2 * T, W)
            gate = _silu(zf_ref[:, kb, :, lanes].astype(F32))
            out_ref[:, kb, :, lanes] = (uf * gate).astype(out_ref.dtype)
            return carry

        lax.fori_loop(0, R, stage3, 0, unroll=FNET_UNROLL)


def _fnet(u5, zf5, w_fnet, tables, batch):
    m1, m2, m3, chan = tables
    R, T, W = RADIX, SUBLANES, FNET_HALF
    return pl.pallas_call(
        _fnet_kernel,
        grid=(batch,),
        in_specs=[pl.BlockSpec((None, R, R, R, FNET_WIDTH), lambda b: (b, 0, 0, 0, 0)),
                  pl.BlockSpec((None, R, R, R, FNET_WIDTH), lambda b: (b, 0, 0, 0, 0)),
                  pl.BlockSpec((FNET_GROUPS, FNET_GDIM, FNET_GDIM), lambda b: (0, 0, 0)),
                  pl.BlockSpec(m1.shape, lambda b: (0, 0)),
                  pl.BlockSpec(m2.shape, lambda b: (0, 0, 0)),
                  pl.BlockSpec(m3.shape, lambda b: (0, 0, 0, 0)),
                  pl.BlockSpec(chan.shape, lambda b: (0, 0))],
        out_specs=pl.BlockSpec((None, R, R, R, FNET_WIDTH), lambda b: (b, 0, 0, 0, 0)),
        out_shape=jax.ShapeDtypeStruct((batch, R, R, R, FNET_WIDTH), BF16),
        scratch_shapes=[pltpu.VMEM((2, R, R, T, W), F32), pltpu.VMEM((2, R, R, T, W), F32),
                        pltpu.VMEM((2, R, R, T, W), F32), pltpu.VMEM((2, R, R, T, W), F32),
                        pltpu.VMEM((FNET_GROUPS, 2 * FNET_GDIM, FNET_GDIM), BF16)],
        compiler_params=pltpu.CompilerParams(dimension_semantics=("arbitrary",),
                                             vmem_limit_bytes=VMEM_LIMIT_BYTES),
        name="fnet",
    )(u5, zf5, w_fnet, m1, m2, m3, chan)


def _out_kernel(x_ref, og_ref, zg_ref, yf_ref, w_ref, lng_ref, lnb_ref, o_ref):
    sub = OUT_ROW_TILE // OUT_SUBTILES
    ys = []
    for t in range(OUT_SUBTILES):
        rows = slice(t * sub, (t + 1) * sub)
        y_gla = (og_ref[rows, :].astype(F32) * _silu(zg_ref[rows, :].astype(F32))).astype(BF16)
        ys.append(_dot(jnp.concatenate([y_gla, yf_ref[rows, :]], axis=1), w_ref[...]))
    for t in range(OUT_SUBTILES):
        rows = slice(t * sub, (t + 1) * sub)
        r = DEEPNORM_ALPHA * x_ref[rows, :] + ys[t]
        mu = jnp.mean(r, axis=-1, keepdims=True)
        d = r - mu
        var = jnp.mean(d * d, axis=-1, keepdims=True)
        o_ref[rows, :] = d * lax.rsqrt(var + LN_EPS) * lng_ref[...] + lnb_ref[...]


def _out_proj(x2d, o_gla, zg, y_fnet, w_out, ln_g, ln_b):
    m = x2d.shape[0]
    row = lambda i: (i, 0)
    const = lambda i: (0, 0)
    return pl.pallas_call(
        _out_kernel,
        grid=(m // OUT_ROW_TILE,),
        in_specs=[pl.BlockSpec((OUT_ROW_TILE, D_MODEL), row),
                  pl.BlockSpec((OUT_ROW_TILE, GLA_WIDTH), row),
                  pl.BlockSpec((OUT_ROW_TILE, GLA_WIDTH), row),
                  pl.BlockSpec((OUT_ROW_TILE, FNET_WIDTH), row),
                  pl.BlockSpec((GLA_WIDTH + FNET_WIDTH, D_MODEL), const),
                  pl.BlockSpec((1, D_MODEL), const),
                  pl.BlockSpec((1, D_MODEL), const)],
        out_specs=pl.BlockSpec((OUT_ROW_TILE, D_MODEL), row),
        out_shape=jax.ShapeDtypeStruct((m, D_MODEL), F32),
        compiler_params=pltpu.CompilerParams(dimension_semantics=("arbitrary",),
                                             vmem_limit_bytes=VMEM_LIMIT_BYTES),
        name="out_proj",
    )(x2d, o_gla, zg, y_fnet, w_out, ln_g, ln_b)


def _permute_w_in(w):
    kw, gw, fw, r = GLA_KEY_WIDTH, GLA_WIDTH, FNET_WIDTH, GATE_RANK
    o_q, o_k, o_v = 0, kw, 2 * kw
    o_gf = o_v + gw
    o_zg = o_gf + 2 * r
    o_u = o_zg + gw
    o_zf = o_u + fw
    gates = jnp.concatenate([w[:, o_gf:o_gf + 2 * r]] * (GATE_PAD // (2 * r)), axis=1)
    cols = [w[:, o_q:o_q + kw] * (GLA_DK ** -0.5), w[:, o_k:o_k + kw], w[:, o_v:o_v + gw], w[:, o_zg:o_zg + gw],
            w[:, o_u:o_u + fw], w[:, o_zf:o_zf + fw], gates]
    return jnp.concatenate(cols, axis=1).astype(BF16)


def kernel(x, w_in, w_gate_up_fwd, b_gate_fwd, w_gate_up_bwd, b_gate_bwd, gla_norm_g,
           w_fnet, w_out, ln_g, ln_b):
    batch, seq, d = x.shape
    assert (seq, d) == (SEQ, D_MODEL) and w_in.shape[0] == DEPTH
    tables = _fnet_tables()
    for l in range(DEPTH):
        x2d = x.reshape(batch * seq, d)
        q, k, v, zg, u, zf, g = _projection(x2d, _permute_w_in(w_in[l]))

        zero = jnp.zeros((GATE_RANK, GLA_KEY_WIDTH), F32)
        w_bd = jnp.block([[w_gate_up_fwd[l], zero], [zero, w_gate_up_bwd[l]]]) * LOG2_E
        w_hi = w_bd.astype(BF16)
        w_lo = (w_bd - w_hi.astype(F32)).astype(BF16)
        wg = jnp.concatenate([w_hi, w_hi, w_lo, jnp.zeros_like(w_hi)], axis=0)
        bg = jnp.concatenate([b_gate_fwd[l], b_gate_bwd[l]])[None, :] * LOG2_E
        o_gla = _gla(q, k, v, g, wg, bg, gla_norm_g[l][None, :], batch)

        seq_digits = (batch, RADIX, RADIX, RADIX, FNET_WIDTH)
        y_fnet = _fnet(u.reshape(seq_digits), zf.reshape(seq_digits), w_fnet[l], tables, batch)

        out = _out_proj(x2d, o_gla, zg, y_fnet.reshape(batch * seq, FNET_WIDTH),
                        w_out[l].astype(BF16), ln_g[l][None, :], ln_b[l][None, :])
        x = out.reshape(batch, seq, d)
    return x
```

```python
import functools
import math

import numpy as np
import jax
import jax.numpy as jnp
from jax import lax
from jax.experimental import pallas as pl
from jax.experimental.pallas import tpu as pltpu

F32 = jnp.float32
BF16 = jnp.bfloat16

D_MODEL = 1024
SEQ = 4096
GLA_HEADS = 4
GLA_DK = 64
GLA_DV = 128
GLA_KEY_WIDTH = GLA_HEADS * GLA_DK
GLA_WIDTH = GLA_HEADS * GLA_DV
GATE_RANK = 16
GATE_TAU = 16.0
FNET_GROUPS = 4
FNET_GDIM = 128
FNET_WIDTH = FNET_GROUPS * FNET_GDIM
LN_EPS = 1e-5
RMS_EPS = 1e-6
DEPTH = 1
DEEPNORM_ALPHA = (2.0 * DEPTH) ** 0.25
LOG2_E = 1.0 / math.log(2.0)

LANES = 128
SUBLANES = 8
VMEM_LIMIT_BYTES = 60 * 1024 * 1024

ROW_TILE = 1024
OUT_ROW_TILE = 2048
OUT_SUBTILES = 8
GLA_CHUNK = 256
GATE_PAD = LANES
RADIX = 16
FNET_HALF = 256
GLA_SAFE_LOG2_DECAY = 100.0
GLA_PIPE_UNROLL = 5
GLA_UNROLL = 8
FNET_UNROLL = 32

_PROJ_COLS = (("q", GLA_KEY_WIDTH), ("k", GLA_KEY_WIDTH), ("v", GLA_WIDTH), ("zg", GLA_WIDTH),
              ("u", FNET_WIDTH), ("zf", FNET_WIDTH), ("g", GATE_PAD))
PROJ_WIDTH = sum(w for _, w in _PROJ_COLS)


def _dot(a, b):
    return jnp.dot(a, b, preferred_element_type=F32)


def _split2(x):
    hi = x.astype(BF16)
    lo = (x - hi.astype(F32)).astype(BF16)
    return hi, lo


def _split3(x):
    hi = x.astype(BF16)
    r = x - hi.astype(F32)
    mid = r.astype(BF16)
    lo = (r - mid.astype(F32)).astype(BF16)
    return hi, mid, lo


def _dot_f32(a, b):
    ah, al = _split2(a)
    bh, bl = _split2(b)
    return _dot(ah, bh) + _dot(ah, bl) + _dot(al, bh)


def _silu(z):
    h = 0.5 * z
    return h + h * jnp.tanh(h)


def _proj_kernel(x_ref, w_ref, q_ref, k_ref, v_ref, zg_ref, u_ref, zf_ref, g_ref):
    xb = x_ref[...].astype(BF16)
    outs = (q_ref, k_ref, v_ref, zg_ref, u_ref, zf_ref, g_ref)
    off = 0
    for (name, width), o_ref in zip(_PROJ_COLS, outs):
        r = _dot(xb, w_ref[:, off:off + width])
        if name == "g":
            hi, lo = _split2(r)
            lane = lax.broadcasted_iota(jnp.int32, r.shape, 1)
            r = jnp.where((lane >= 2 * GATE_RANK) & (lane < 4 * GATE_RANK), lo, hi)
        o_ref[...] = r.astype(o_ref.dtype)
        off += width


def _projection(x2d, w_perm):
    m = x2d.shape[0]
    grid = (m // ROW_TILE,)
    row = lambda i: (i, 0)
    out_shape = [jax.ShapeDtypeStruct((m, w), BF16) for _, w in _PROJ_COLS]
    out_specs = [pl.BlockSpec((ROW_TILE, w), row) for _, w in _PROJ_COLS]
    return pl.pallas_call(
        _proj_kernel,
        grid=grid,
        in_specs=[pl.BlockSpec((ROW_TILE, D_MODEL), row),
                  pl.BlockSpec((D_MODEL, PROJ_WIDTH), lambda i: (0, 0))],
        out_specs=out_specs,
        out_shape=out_shape,
        compiler_params=pltpu.CompilerParams(dimension_semantics=("arbitrary",),
                                             vmem_limit_bytes=VMEM_LIMIT_BYTES),
        name="in_proj",
    )(x2d, w_perm)


def _gla_kernel(q_ref, k_ref, v_ref, g_ref, wg_ref, bg_ref, gam_ref, y_ref,
                oacc_ref, qd_ref, ki_ref, kdt_ref, upd_ref, dec_ref, dect_ref,
                flag_ref, bsc_ref, qsc_ref, psc_ref):
    C = GLA_CHUNK
    KW = GLA_KEY_WIDTH
    n_chunks = SEQ // C
    row_i = lax.broadcasted_iota(jnp.int32, (C, C), 0)
    col_i = lax.broadcasted_iota(jnp.int32, (C, C), 1)
    lower = row_i >= col_i
    upper = row_i <= col_i
    lane_head = lax.broadcasted_iota(jnp.int32, (C, KW), 1) // GLA_DK
    state_head = lax.broadcasted_iota(jnp.int32, (KW, GLA_DV), 0) // GLA_DK
    pair_ones = (lax.broadcasted_iota(jnp.int32, (2 * GLA_DV, 2 * GLA_DV), 0) // GLA_DV ==
                 lax.broadcasted_iota(jnp.int32, (2 * GLA_DV, 2 * GLA_DV), 1) // GLA_DV).astype(BF16)

    def chunk_rows(n):
        return pl.ds(pl.multiple_of(n * C, C), C)

    def gate_preact(rows):
        return _dot(g_ref[rows, :], wg_ref[...]) + bg_ref[...]

    def gate_logs(z2):
        tau_log2_a = jnp.minimum(z2, 0.0) - jnp.log2(1.0 + jnp.exp2(-jnp.abs(z2)))
        hi = tau_log2_a.astype(BF16)
        mid = (tau_log2_a - hi.astype(F32)).astype(BF16)
        return hi, mid

    def cum_log_decays(hi, mid):
        lo_t = jnp.where(lower, 1.0 / GATE_TAU, 0.0).astype(BF16)
        up_t = jnp.where(upper, 1.0 / GATE_TAU, 0.0).astype(BF16)
        b_f = _dot(jnp.concatenate([lo_t, lo_t], axis=1),
                   jnp.concatenate([hi[:, :KW], mid[:, :KW]], axis=0))
        b_b = _dot(jnp.concatenate([up_t, up_t], axis=1),
                   jnp.concatenate([hi[:, KW:], mid[:, KW:]], axis=0))
        return b_f, b_b

    def scale_qk(n, b_f, b_b):
        rows = chunk_rows(n)
        e_f = b_f[C - 1:C, :]
        e_b = b_b[0:1, :]
        dec_ref[n, 0] = jnp.broadcast_to(jnp.exp2(e_f), (SUBLANES, KW))
        dec_ref[n, 1] = jnp.broadcast_to(jnp.exp2(e_b), (SUBLANES, KW))
        flag_ref[n] = (jnp.max(-jnp.minimum(e_f, e_b)) > GLA_SAFE_LOG2_DECAY).astype(jnp.int32)

        qf = q_ref[rows, :].astype(F32)
        kf = k_ref[rows, :].astype(F32)
        qd_ref[rows, :KW] = (qf * jnp.exp2(b_f)).astype(BF16)
        qd_ref[rows, KW:] = (qf * jnp.exp2(b_b)).astype(BF16)
        ki_ref[rows, :KW] = (kf * jnp.exp2(-b_f)).astype(BF16)
        ki_ref[rows, KW:] = (kf * jnp.exp2(-b_b)).astype(BF16)
        kd = jnp.concatenate([kf * jnp.exp2(e_f - b_f), kf * jnp.exp2(e_b - b_b)], axis=1)
        kdt_ref[n] = jnp.transpose(kd).astype(BF16)

    def scores(n):
        rows = chunk_rows(n)
        nt = (((1,), (1,)), ((), ()))
        qd_f = qd_ref[rows, :KW]
        qd_b = qd_ref[rows, KW:]
        probs = []
        for h in range(GLA_HEADS):
            in_head = lane_head == h
            s_f = lax.dot_general(jnp.where(in_head, qd_f, jnp.zeros_like(qd_f)), ki_ref[rows, :KW],
                                  nt, preferred_element_type=F32)
            s_b = lax.dot_general(jnp.where(in_head, qd_b, jnp.zeros_like(qd_b)), ki_ref[rows, KW:],
                                  nt, preferred_element_type=F32)
            probs.append((jnp.where(lower, s_f, 0.0) + jnp.where(upper, s_b, 0.0)).astype(BF16))
        return probs

    def weigh_values(n, probs):
        rows = chunk_rows(n)
        for h in range(GLA_HEADS):
            vcols = slice(h * GLA_DV, (h + 1) * GLA_DV)
            lhs = jnp.concatenate([probs[h], kdt_ref[n, h * GLA_DK:(h + 1) * GLA_DK, :],
                                   kdt_ref[n, KW + h * GLA_DK:KW + (h + 1) * GLA_DK, :]], axis=0)
            r = _dot(lhs, v_ref[rows, vcols])
            oacc_ref[rows, vcols] = r[:C]
            upd_ref[n, 0, h * GLA_DK:(h + 1) * GLA_DK, :] = r[C:C + GLA_DK]
            upd_ref[n, 1, h * GLA_DK:(h + 1) * GLA_DK, :] = r[C + GLA_DK:]
        for d in range(2):
            dect_ref[n, d] = jnp.transpose(jnp.broadcast_to(dec_ref[n, d][0:1, :], (LANES, KW)))

    def attend_and_scale_next(n, carry):
        z2 = gate_preact(chunk_rows(n + 1))
        probs = scores(n)
        b_next = cum_log_decays(*gate_logs(z2))
        weigh_values(n, probs)
        scale_qk(n + 1, *b_next)
        return carry

    scale_qk(0, *cum_log_decays(*gate_logs(gate_preact(chunk_rows(0)))))
    lax.fori_loop(0, n_chunks - 1, attend_and_scale_next, 0, unroll=GLA_PIPE_UNROLL)
    weigh_values(n_chunks - 1, scores(n_chunks - 1))

    def redo_exact(n, carry):
        @pl.when(flag_ref[n] != 0)
        def _():
            rows = pl.ds(pl.multiple_of(n * C, C), C)
            nt = (((1,), (1,)), ((), ()))
            b_f, b_b = cum_log_decays(*gate_logs(gate_preact(rows)))
            bsc_ref[:, :KW] = b_f
            bsc_ref[:, KW:] = b_b
            qsc_ref[...] = q_ref[rows, :].astype(F32)
            col = lax.broadcasted_iota(jnp.int32, (SUBLANES, C), 1)
            head_row = (lax.broadcasted_iota(jnp.int32, (SUBLANES, KW), 0) ==
                        lax.broadcasted_iota(jnp.int32, (SUBLANES, KW), 1) // GLA_DK)

            def row_scores(i, c):
                kf = k_ref[rows, :].astype(F32)
                q_heads = jnp.where(head_row, jnp.broadcast_to(qsc_ref[pl.ds(i, 1), :], (SUBLANES, KW)),
                                    0.0).astype(BF16)
                w_f = jnp.exp2(jnp.minimum(bsc_ref[pl.ds(i, 1), :KW] - bsc_ref[:, :KW], 0.0))
                w_b = jnp.exp2(jnp.minimum(bsc_ref[pl.ds(i, 1), KW:] - bsc_ref[:, KW:], 0.0))
                s_f = lax.dot_general(q_heads, (kf * w_f).astype(BF16), nt, preferred_element_type=F32)
                s_b = lax.dot_general(q_heads, (kf * w_b).astype(BF16), nt, preferred_element_type=F32)
                p = jnp.where(col <= i, s_f, 0.0) + jnp.where(col >= i, s_b, 0.0)
                for h in range(GLA_HEADS):
                    psc_ref[h, pl.ds(i, 1), :] = p[h:h + 1, :]
                return c

            lax.fori_loop(0, C, row_scores, 0)
            for h in range(GLA_HEADS):
                vcols = slice(h * GLA_DV, (h + 1) * GLA_DV)
                oacc_ref[rows, vcols] = _dot(psc_ref[h].astype(BF16), v_ref[rows, vcols])
        return carry

    lax.fori_loop(0, n_chunks, redo_exact, 0)

    def expand(state):
        sb = state.astype(BF16)
        return jnp.concatenate([jnp.where(state_head == h, sb, jnp.zeros_like(sb))
                                for h in range(GLA_HEADS)], axis=1)

    def advance(n, d, state, qcols):
        rows = pl.ds(pl.multiple_of(n * C, C), C)
        oacc_ref[rows, :] += _dot(qd_ref[rows, qcols], expand(state))
        return state * dect_ref[n, d] + upd_ref[n, d]

    def finalize(n):
        rows = pl.ds(pl.multiple_of(n * C, C), C)
        for pair in range(GLA_HEADS // 2):
            cols = slice(2 * pair * GLA_DV, 2 * (pair + 1) * GLA_DV)
            o_p = oacc_ref[rows, cols]
            ms = _dot((o_p * o_p).astype(BF16), pair_ones) * (1.0 / GLA_DV)
            y_ref[rows, cols] = (o_p * lax.rsqrt(ms + RMS_EPS) * gam_ref[:, cols]).astype(y_ref.dtype)

    def inter(j, states, done):
        s_f, s_b = states
        m = n_chunks - 1 - j
        s_f = advance(j, 0, s_f, slice(0, KW))
        s_b = advance(m, 1, s_b, slice(KW, 2 * KW))
        if done:
            finalize(j)
            finalize(m)
        return s_f, s_b

    state0 = jnp.zeros((KW, GLA_DV), F32)
    states = lax.fori_loop(0, n_chunks // 2, lambda j, st: inter(j, st, False), (state0, state0),
                           unroll=GLA_UNROLL)
    lax.fori_loop(n_chunks // 2, n_chunks, lambda j, st: inter(j, st, True), states,
                  unroll=GLA_UNROLL)


def _gla(q, k, v, g, wg, bg, gamma, batch):
    tok = lambda b: (b, 0)
    const = lambda b: (0, 0)
    n_chunks = SEQ // GLA_CHUNK
    return pl.pallas_call(
        _gla_kernel,
        grid=(batch,),
        in_specs=[pl.BlockSpec((SEQ, GLA_KEY_WIDTH), tok),
                  pl.BlockSpec((SEQ, GLA_KEY_WIDTH), tok),
                  pl.BlockSpec((SEQ, GLA_WIDTH), tok),
                  pl.BlockSpec((SEQ, GATE_PAD), tok),
                  pl.BlockSpec((GATE_PAD, 2 * GLA_KEY_WIDTH), const),
                  pl.BlockSpec((1, 2 * GLA_KEY_WIDTH), const),
                  pl.BlockSpec((1, GLA_WIDTH), const)],
        out_specs=pl.BlockSpec((SEQ, GLA_WIDTH), tok),
        out_shape=jax.ShapeDtypeStruct((batch * SEQ, GLA_WIDTH), BF16),
        scratch_shapes=[pltpu.VMEM((SEQ, GLA_WIDTH), F32),
                        pltpu.VMEM((SEQ, 2 * GLA_KEY_WIDTH), BF16),
                        pltpu.VMEM((SEQ, 2 * GLA_KEY_WIDTH), BF16),
                        pltpu.VMEM((n_chunks, 2 * GLA_KEY_WIDTH, GLA_CHUNK), BF16),
                        pltpu.VMEM((n_chunks, 2, GLA_KEY_WIDTH, GLA_DV), F32),
                        pltpu.VMEM((n_chunks, 2, SUBLANES, GLA_KEY_WIDTH), F32),
                        pltpu.VMEM((n_chunks, 2, GLA_KEY_WIDTH, LANES), F32),
                        pltpu.SMEM((n_chunks,), jnp.int32),
                        pltpu.VMEM((GLA_CHUNK, 2 * GLA_KEY_WIDTH), F32),
                        pltpu.VMEM((GLA_CHUNK, GLA_KEY_WIDTH), F32),
                        pltpu.VMEM((GLA_HEADS, GLA_CHUNK, GLA_CHUNK), F32)],
        compiler_params=pltpu.CompilerParams(dimension_semantics=("arbitrary",),
                                             vmem_limit_bytes=VMEM_LIMIT_BYTES),
        name="gla",
    )(q, k, v, g, wg, bg, gamma)


def _fnet_tables():
    R = RADIX
    n = SEQ
    a = np.arange(R)
    lo8 = np.arange(SUBLANES)
    eye8 = np.eye(SUBLANES)

    ka = (8 * np.arange(2)[:, None] + lo8[None, :])
    ang = 2 * np.pi * (256 * a[None, None, :] * ka[:, :, None]) / n
    eye16 = np.eye(R)
    def s1(trig):
        t = trig(ang)
        m = t[:, None, :, :, None] * eye16[None, :, None, None, :]
        return m.reshape(2 * R * SUBLANES, R * R)
    m1 = np.concatenate([s1(np.cos), -s1(np.sin)], axis=0)

    def complex_block(theta):
        c = np.cos(theta)
        s = -np.sin(theta)
        def kron(t):
            m = t[:, :, :, None] * eye8[None, :, None, :]
            return m.reshape(t.shape[0] * SUBLANES, t.shape[2] * SUBLANES)
        mr, mi = kron(c), kron(s)
        return np.block([[mr, -mi], [mi, mr]])

    m2 = []
    for kah in range(2):
        k_a = 8 * kah + lo8
        expo = 16 * a[None, None, :] * k_a[None, :, None] + 256 * a[None, None, :] * a[:, None, None]
        m2.append(complex_block(2 * np.pi * expo / n))
    m2 = np.stack(m2)

    m3 = np.zeros((2, R, 2 * R * SUBLANES, 2 * R * SUBLANES))
    for kah in range(2):
        k_a = 8 * kah + lo8
        for kb in range(R):
            expo = (a[None, None, :] * k_a[None, :, None] + 16 * a[None, None, :] * kb
                    + 256 * a[None, None, :] * a[:, None, None])
            m3[kah, kb] = complex_block(2 * np.pi * expo / n)

    c = np.arange(FNET_GDIM)
    ang_c = 2 * np.pi * np.outer(c, c) / FNET_GDIM
    norm = 1.0 / math.sqrt(SEQ * FNET_GDIM)
    chan = np.concatenate([np.cos(ang_c), np.sin(ang_c)], axis=0) * norm
    return (jnp.asarray(m1, F32).astype(BF16), jnp.asarray(m2, F32).astype(BF16),
            jnp.asarray(m3, F32).astype(BF16), jnp.asarray(chan, F32))


def _fnet_kernel(u_ref, zf_ref, wf_ref, m1_ref, m2_ref, m3_ref, chan_ref, out_ref,
                 y1r, y1i, y2r, y2i, mg_ref):
    R = RADIX
    T = SUBLANES
    W = FNET_HALF

    @pl.when(pl.program_id(0) == 0)
    def _():
        for g in range(FNET_GROUPS):
            mg_ref[g] = _dot_f32(chan_ref[...], wf_ref[g]).astype(BF16)

    for half in range(FNET_WIDTH // W):
        lanes = slice(half * W, (half + 1) * W)

        def stage1(sm, carry):
            blk = u_ref[:, sm, :, lanes].reshape(R * R, W)
            res = _dot(m1_ref[...], blk)
            y1r[:, :, sm, :, :] = res[:R * R].reshape(2, R, T, W)
            y1i[:, :, sm, :, :] = res[R * R:].reshape(2, R, T, W)
            return carry

        lax.fori_loop(0, R, stage1, 0, unroll=FNET_UNROLL)

        def stage2(i, carry):
            kah = i // R
            sl = i % R
            rhs = jnp.concatenate([y1r[kah, sl].reshape(R * T, W), y1i[kah, sl].reshape(R * T, W)],
                                  axis=0).astype(BF16)
            res = _dot(m2_ref[kah], rhs)
            y2r[kah, :, sl, :, :] = res[:R * T].reshape(R, T, W)
            y2i[kah, :, sl, :, :] = res[R * T:].reshape(R, T, W)
            return carry

        lax.fori_loop(0, 2 * R, stage2, 0, unroll=FNET_UNROLL)

        def stage3(kb, carry):
            mixed = []
            for kah in range(2):
                rhs = jnp.concatenate([y2r[kah, kb].reshape(R * T, W), y2i[kah, kb].reshape(R * T, W)],
                                      axis=0).astype(BF16)
                res = _dot(m3_ref[kah, kb], rhs)
                groups = []
                for j in range(W // FNET_GDIM):
                    gl = slice(j * FNET_GDIM, (j + 1) * FNET_GDIM)
                    z = jnp.concatenate([res[:R * T, gl], res[R * T:, gl]], axis=1).astype(BF16)
                    groups.append(_dot(z, mg_ref[half * (W // FNET_GDIM) + j]))
                mixed.append(jnp.concatenate(groups, axis=1).reshape(R, T, W))
            uf = jnp.stack(mixed, axis=1).reshape(R, 2 * T, W)
            gate = _silu(zf_ref[:, kb, :, lanes].astype(F32))
            out_ref[:, kb, :, lanes] = (uf * gate).astype(out_ref.dtype)
            return carry

        lax.fori_loop(0, R, stage3, 0, unroll=FNET_UNROLL)


def _fnet(u5, zf5, w_fnet, tables, batch):
    m1, m2, m3, chan = tables
    R, T, W = RADIX, SUBLANES, FNET_HALF
    return pl.pallas_call(
        _fnet_kernel,
        grid=(batch,),
        in_specs=[pl.BlockSpec((None, R, R, R, FNET_WIDTH), lambda b: (b, 0, 0, 0, 0)),
                  pl.BlockSpec((None, R, R, R, FNET_WIDTH), lambda b: (b, 0, 0, 0, 0)),
                  pl.BlockSpec((FNET_GROUPS, FNET_GDIM, FNET_GDIM), lambda b: (0, 0, 0)),
                  pl.BlockSpec(m1.shape, lambda b: (0, 0)),
                  pl.BlockSpec(m2.shape, lambda b: (0, 0, 0)),
                  pl.BlockSpec(m3.shape, lambda b: (0, 0, 0, 0)),
                  pl.BlockSpec(chan.shape, lambda b: (0, 0))],
        out_specs=pl.BlockSpec((None, R, R, R, FNET_WIDTH), lambda b: (b, 0, 0, 0, 0)),
        out_shape=jax.ShapeDtypeStruct((batch, R, R, R, FNET_WIDTH), BF16),
        scratch_shapes=[pltpu.VMEM((2, R, R, T, W), F32), pltpu.VMEM((2, R, R, T, W), F32),
                        pltpu.VMEM((2, R, R, T, W), F32), pltpu.VMEM((2, R, R, T, W), F32),
                        pltpu.VMEM((FNET_GROUPS, 2 * FNET_GDIM, FNET_GDIM), BF16)],
        compiler_params=pltpu.CompilerParams(dimension_semantics=("arbitrary",),
                                             vmem_limit_bytes=VMEM_LIMIT_BYTES),
        name="fnet",
    )(u5, zf5, w_fnet, m1, m2, m3, chan)


def _out_kernel(x_ref, og_ref, zg_ref, yf_ref, w_ref, lng_ref, lnb_ref, o_ref):
    sub = OUT_ROW_TILE // OUT_SUBTILES
    ys = []
    for t in range(OUT_SUBTILES):
        rows = slice(t * sub, (t + 1) * sub)
        y_gla = (og_ref[rows, :].astype(F32) * _silu(zg_ref[rows, :].astype(F32))).astype(BF16)
        ys.append(_dot(jnp.concatenate([y_gla, yf_ref[rows, :]], axis=1), w_ref[...]))
    for t in range(OUT_SUBTILES):
        rows = slice(t * sub, (t + 1) * sub)
        r = DEEPNORM_ALPHA * x_ref[rows, :] + ys[t]
        mu = jnp.mean(r, axis=-1, keepdims=True)
        d = r - mu
        var = jnp.mean(d * d, axis=-1, keepdims=True)
        o_ref[rows, :] = d * lax.rsqrt(var + LN_EPS) * lng_ref[...] + lnb_ref[...]


def _out_proj(x2d, o_gla, zg, y_fnet, w_out, ln_g, ln_b):
    m = x2d.shape[0]
    row = lambda i: (i, 0)
    const = lambda i: (0, 0)
    return pl.pallas_call(
        _out_kernel,
        grid=(m // OUT_ROW_TILE,),
        in_specs=[pl.BlockSpec((OUT_ROW_TILE, D_MODEL), row),
                  pl.BlockSpec((OUT_ROW_TILE, GLA_WIDTH), row),
                  pl.BlockSpec((OUT_ROW_TILE, GLA_WIDTH), row),
                  pl.BlockSpec((OUT_ROW_TILE, FNET_WIDTH), row),
                  pl.BlockSpec((GLA_WIDTH + FNET_WIDTH, D_MODEL), const),
                  pl.BlockSpec((1, D_MODEL), const),
                  pl.BlockSpec((1, D_MODEL), const)],
        out_specs=pl.BlockSpec((OUT_ROW_TILE, D_MODEL), row),
        out_shape=jax.ShapeDtypeStruct((m, D_MODEL), F32),
        compiler_params=pltpu.CompilerParams(dimension_semantics=("arbitrary",),
                                             vmem_limit_bytes=VMEM_LIMIT_BYTES),
        name="out_proj",
    )(x2d, o_gla, zg, y_fnet, w_out, ln_g, ln_b)


def _permute_w_in(w):
    kw, gw, fw, r = GLA_KEY_WIDTH, GLA_WIDTH, FNET_WIDTH, GATE_RANK
    o_q, o_k, o_v = 0, kw, 2 * kw
    o_gf = o_v + gw
    o_zg = o_gf + 2 * r
    o_u = o_zg + gw
    o_zf = o_u + fw
    gates = jnp.concatenate([w[:, o_gf:o_gf + 2 * r]] * (GATE_PAD // (2 * r)), axis=1)
    cols = [w[:, o_q:o_q + kw] * (GLA_DK ** -0.5), w[:, o_k:o_k + kw], w[:, o_v:o_v + gw], w[:, o_zg:o_zg + gw],
            w[:, o_u:o_u + fw], w[:, o_zf:o_zf + fw], gates]
    return jnp.concatenate(cols, axis=1).astype(BF16)


def kernel(x, w_in, w_gate_up_fwd, b_gate_fwd, w_gate_up_bwd, b_gate_bwd, gla_norm_g,
           w_fnet, w_out, ln_g, ln_b):
    batch, seq, d = x.shape
    assert (seq, d) == (SEQ, D_MODEL) and w_in.shape[0] == DEPTH
    tables = _fnet_tables()
    for l in range(DEPTH):
        x2d = x.reshape(batch * seq, d)
        q, k, v, zg, u, zf, g = _projection(x2d, _permute_w_in(w_in[l]))

        zero = jnp.zeros((GATE_RANK, GLA_KEY_WIDTH), F32)
        w_bd = jnp.block([[w_gate_up_fwd[l], zero], [zero, w_gate_up_bwd[l]]]) * LOG2_E
        w_hi = w_bd.astype(BF16)
        w_lo = (w_bd - w_hi.astype(F32)).astype(BF16)
        wg = jnp.concatenate([w_hi, w_hi, w_lo, jnp.zeros_like(w_hi)], axis=0)
        bg = jnp.concatenate([b_gate_fwd[l], b_gate_bwd[l]])[None, :] * LOG2_E
        o_gla = _gla(q, k, v, g, wg, bg, gla_norm_g[l][None, :], batch)

        seq_digits = (batch, RADIX, RADIX, RADIX, FNET_WIDTH)
        y_fnet = _fnet(u.reshape(seq_digits), zf.reshape(seq_digits), w_fnet[l], tables, batch)

        out = _out_proj(x2d, o_gla, zg, y_fnet.reshape(batch * seq, FNET_WIDTH),
                        w_out[l].astype(BF16), ln_g[l][None, :], ln_b[l][None, :])
        x = out.reshape(batch, seq, d)
    return x
```

```python
import functools
import math

import numpy as np
import jax
import jax.numpy as jnp
from jax import lax
from jax.experimental import pallas as pl
from jax.experimental.pallas import tpu as pltpu

F32 = jnp.float32
BF16 = jnp.bfloat16

D_MODEL = 1024
SEQ = 4096
GLA_HEADS = 4
GLA_DK = 64
GLA_DV = 128
GLA_KEY_WIDTH = GLA_HEADS * GLA_DK
GLA_WIDTH = GLA_HEADS * GLA_DV
GATE_RANK = 16
GATE_TAU = 16.0
FNET_GROUPS = 4
FNET_GDIM = 128
FNET_WIDTH = FNET_GROUPS * FNET_GDIM
LN_EPS = 1e-5
RMS_EPS = 1e-6
DEPTH = 1
DEEPNORM_ALPHA = (2.0 * DEPTH) ** 0.25
LOG2_E = 1.0 / math.log(2.0)

LANES = 128
SUBLANES = 8
VMEM_LIMIT_BYTES = 60 * 1024 * 1024

ROW_TILE = 2048
OUT_ROW_TILE = 2048
OUT_SUBTILES = 8
GLA_CHUNK = 256
GATE_PAD = LANES
RADIX = 16
FNET_HALF = 256
GLA_SAFE_LOG2_DECAY = 100.0
GLA_PIPE_UNROLL = 15
GLA_UNROLL = 8
FNET_UNROLL = 32

_PROJ_COLS = (("q", GLA_KEY_WIDTH), ("k", GLA_KEY_WIDTH), ("v", GLA_WIDTH), ("zg", GLA_WIDTH),
              ("u", FNET_WIDTH), ("zf", FNET_WIDTH), ("g", GATE_PAD))
PROJ_WIDTH = sum(w for _, w in _PROJ_COLS)


def _dot(a, b):
    return jnp.dot(a, b, preferred_element_type=F32)


def _split2(x):
    hi = x.astype(BF16)
    lo = (x - hi.astype(F32)).astype(BF16)
    return hi, lo


def _split3(x):
    hi = x.astype(BF16)
    r = x - hi.astype(F32)
    mid = r.astype(BF16)
    lo = (r - mid.astype(F32)).astype(BF16)
    return hi, mid, lo


def _dot_f32(a, b):
    ah, al = _split2(a)
    bh, bl = _split2(b)
    return _dot(ah, bh) + _dot(ah, bl) + _dot(al, bh)


def _silu(z):
    h = 0.5 * z
    return h + h * jnp.tanh(h)


def _proj_kernel(x_ref, w_ref, q_ref, k_ref, v_ref, zg_ref, u_ref, zf_ref, g_ref):
    xb = x_ref[...].astype(BF16)
    outs = (q_ref, k_ref, v_ref, zg_ref, u_ref, zf_ref, g_ref)
    off = 0
    for (name, width), o_ref in zip(_PROJ_COLS, outs):
        r = _dot(xb, w_ref[:, off:off + width])
        if name == "g":
            hi, lo = _split2(r)
            lane = lax.broadcasted_iota(jnp.int32, r.shape, 1)
            r = jnp.where((lane >= 2 * GATE_RANK) & (lane < 4 * GATE_RANK), lo, hi)
        o_ref[...] = r.astype(o_ref.dtype)
        off += width


def _projection(x2d, w_perm):
    m = x2d.shape[0]
    grid = (m // ROW_TILE,)
    row = lambda i: (i, 0)
    out_shape = [jax.ShapeDtypeStruct((m, w), BF16) for _, w in _PROJ_COLS]
    out_specs = [pl.BlockSpec((ROW_TILE, w), row) for _, w in _PROJ_COLS]
    return pl.pallas_call(
        _proj_kernel,
        grid=grid,
        in_specs=[pl.BlockSpec((ROW_TILE, D_MODEL), row),
                  pl.BlockSpec((D_MODEL, PROJ_WIDTH), lambda i: (0, 0))],
        out_specs=out_specs,
        out_shape=out_shape,
        compiler_params=pltpu.CompilerParams(dimension_semantics=("arbitrary",),
                                             vmem_limit_bytes=VMEM_LIMIT_BYTES),
        name="in_proj",
    )(x2d, w_perm)


def _gla_kernel(q_ref, k_ref, v_ref, g_ref, wg_ref, bg_ref, gam_ref, y_ref,
                oacc_ref, qd_ref, ki_ref, kdt_ref, upd_ref, dec_ref, dect_ref,
                flag_ref, bsc_ref, qsc_ref, psc_ref):
    C = GLA_CHUNK
    KW = GLA_KEY_WIDTH
    n_chunks = SEQ // C
    row_i = lax.broadcasted_iota(jnp.int32, (C, C), 0)
    col_i = lax.broadcasted_iota(jnp.int32, (C, C), 1)
    lower = row_i >= col_i
    upper = row_i <= col_i
    lane_head = lax.broadcasted_iota(jnp.int32, (C, KW), 1) // GLA_DK
    pair_top = lax.broadcasted_iota(jnp.int32, (2 * GLA_DK, GLA_DV), 0) < GLA_DK
    same_head = (lax.broadcasted_iota(jnp.int32, (2 * GLA_DV, 2 * GLA_DV), 0) // GLA_DV ==
                 lax.broadcasted_iota(jnp.int32, (2 * GLA_DV, 2 * GLA_DV), 1) // GLA_DV)
    pair_mean = jnp.where(same_head, 1.0 / GLA_DV, 0.0).astype(BF16)

    def chunk_rows(n):
        return pl.ds(pl.multiple_of(n * C, C), C)

    def gate_preact(rows):
        return _dot(g_ref[rows, :], wg_ref[...]) + bg_ref[...]

    def gate_logs(z2):
        tau_log2_a = jnp.minimum(z2, 0.0) - jnp.log2(1.0 + jnp.exp2(-jnp.abs(z2)))
        hi = tau_log2_a.astype(BF16)
        mid = (tau_log2_a - hi.astype(F32)).astype(BF16)
        return hi, mid

    def cum_log_decays(hi, mid):
        lo_t = jnp.where(lower, 1.0 / GATE_TAU, 0.0).astype(BF16)
        up_t = jnp.where(upper, 1.0 / GATE_TAU, 0.0).astype(BF16)
        b_f = _dot(jnp.concatenate([lo_t, lo_t], axis=1),
                   jnp.concatenate([hi[:, :KW], mid[:, :KW]], axis=0))
        b_b = _dot(jnp.concatenate([up_t, up_t], axis=1),
                   jnp.concatenate([hi[:, KW:], mid[:, KW:]], axis=0))
        return b_f, b_b

    def scale_qk(n, b_f, b_b):
        rows = chunk_rows(n)
        e_f = b_f[C - 1:C, :]
        e_b = b_b[0:1, :]
        dec_ref[n, 0] = jnp.broadcast_to(jnp.exp2(e_f), (SUBLANES, KW))
        dec_ref[n, 1] = jnp.broadcast_to(jnp.exp2(e_b), (SUBLANES, KW))
        flag_ref[n] = (jnp.max(-jnp.minimum(e_f, e_b)) > GLA_SAFE_LOG2_DECAY).astype(jnp.int32)

        qf = q_ref[rows, :].astype(F32)
        kf = k_ref[rows, :].astype(F32)
        qd_ref[rows, :KW] = (qf * jnp.exp2(b_f)).astype(BF16)
        qd_ref[rows, KW:] = (qf * jnp.exp2(b_b)).astype(BF16)
        ki_ref[rows, :KW] = (kf * jnp.exp2(-b_f)).astype(BF16)
        ki_ref[rows, KW:] = (kf * jnp.exp2(-b_b)).astype(BF16)
        kd = jnp.concatenate([kf * jnp.exp2(e_f - b_f), kf * jnp.exp2(e_b - b_b)], axis=1)
        kdt_ref[n] = jnp.transpose(kd).astype(BF16)

    def scores(n):
        rows = chunk_rows(n)
        nt = (((1,), (1,)), ((), ()))
        qd_f = qd_ref[rows, :KW]
        qd_b = qd_ref[rows, KW:]
        probs = []
        for h in range(GLA_HEADS):
            in_head = lane_head == h
            s_f = lax.dot_general(jnp.where(in_head, qd_f, jnp.zeros_like(qd_f)), ki_ref[rows, :KW],
                                  nt, preferred_element_type=F32)
            s_b = lax.dot_general(jnp.where(in_head, qd_b, jnp.zeros_like(qd_b)), ki_ref[rows, KW:],
                                  nt, preferred_element_type=F32)
            probs.append((jnp.where(lower, s_f, 0.0) + jnp.where(upper, s_b, 0.0)).astype(BF16))
        return probs

    def weigh_values(n, probs):
        rows = chunk_rows(n)
        for h in range(GLA_HEADS):
            vcols = slice(h * GLA_DV, (h + 1) * GLA_DV)
            lhs = jnp.concatenate([probs[h], kdt_ref[n, h * GLA_DK:(h + 1) * GLA_DK, :],
                                   kdt_ref[n, KW + h * GLA_DK:KW + (h + 1) * GLA_DK, :]], axis=0)
            r = _dot(lhs, v_ref[rows, vcols])
            oacc_ref[rows, vcols] = r[:C]
            upd_ref[n, 0, h * GLA_DK:(h + 1) * GLA_DK, :] = r[C:C + GLA_DK]
            upd_ref[n, 1, h * GLA_DK:(h + 1) * GLA_DK, :] = r[C + GLA_DK:]
        for d in range(2):
            dect_ref[n, d] = jnp.transpose(jnp.broadcast_to(dec_ref[n, d][0:1, :], (LANES, KW)))

    def attend_and_scale_next(n, carry):
        z2 = gate_preact(chunk_rows(n + 1))
        probs = scores(n)
        b_next = cum_log_decays(*gate_logs(z2))
        weigh_values(n, probs)
        scale_qk(n + 1, *b_next)
        return carry

    scale_qk(0, *cum_log_decays(*gate_logs(gate_preact(chunk_rows(0)))))
    lax.fori_loop(0, n_chunks - 1, attend_and_scale_next, 0, unroll=GLA_PIPE_UNROLL)
    weigh_values(n_chunks - 1, scores(n_chunks - 1))

    def redo_exact(n, carry):
        @pl.when(flag_ref[n] != 0)
        def _():
            rows = pl.ds(pl.multiple_of(n * C, C), C)
            nt = (((1,), (1,)), ((), ()))
            b_f, b_b = cum_log_decays(*gate_logs(gate_preact(rows)))
            bsc_ref[:, :KW] = b_f
            bsc_ref[:, KW:] = b_b
            qsc_ref[...] = q_ref[rows, :].astype(F32)
            col = lax.broadcasted_iota(jnp.int32, (SUBLANES, C), 1)
            head_row = (lax.broadcasted_iota(jnp.int32, (SUBLANES, KW), 0) ==
                        lax.broadcasted_iota(jnp.int32, (SUBLANES, KW), 1) // GLA_DK)

            def row_scores(i, c):
                kf = k_ref[rows, :].astype(F32)
                q_heads = jnp.where(head_row, jnp.broadcast_to(qsc_ref[pl.ds(i, 1), :], (SUBLANES, KW)),
                                    0.0).astype(BF16)
                w_f = jnp.exp2(jnp.minimum(bsc_ref[pl.ds(i, 1), :KW] - bsc_ref[:, :KW], 0.0))
                w_b = jnp.exp2(jnp.minimum(bsc_ref[pl.ds(i, 1), KW:] - bsc_ref[:, KW:], 0.0))
                s_f = lax.dot_general(q_heads, (kf * w_f).astype(BF16), nt, preferred_element_type=F32)
                s_b = lax.dot_general(q_heads, (kf * w_b).astype(BF16), nt, preferred_element_type=F32)
                p = jnp.where(col <= i, s_f, 0.0) + jnp.where(col >= i, s_b, 0.0)
                for h in range(GLA_HEADS):
                    psc_ref[h, pl.ds(i, 1), :] = p[h:h + 1, :]
                return c

            lax.fori_loop(0, C, row_scores, 0)
            for h in range(GLA_HEADS):
                vcols = slice(h * GLA_DV, (h + 1) * GLA_DV)
                oacc_ref[rows, vcols] = _dot(psc_ref[h].astype(BF16), v_ref[rows, vcols])
        return carry

    lax.fori_loop(0, n_chunks, redo_exact, 0)

    def enter_state(n, d, state):
        increment = upd_ref[n, d]
        upd_ref[n, d] = state
        return state * dect_ref[n, d] + increment

    def scan_states(j, states):
        s_f, s_b = states
        return enter_state(j, 0, s_f), enter_state(n_chunks - 1 - j, 1, s_b)

    state0 = jnp.zeros((KW, GLA_DV), F32)
    lax.fori_loop(0, n_chunks, scan_states, (state0, state0), unroll=GLA_UNROLL)

    def pair_operand(state_rows):
        sb = state_rows.astype(BF16)
        zero = jnp.zeros_like(sb)
        return jnp.concatenate([jnp.where(pair_top, sb, zero), jnp.where(pair_top, zero, sb)], axis=1)

    def cross_chunk_and_norm(n, carry):
        rows = chunk_rows(n)
        for pair in range(GLA_HEADS // 2):
            kdims = slice(2 * pair * GLA_DK, 2 * (pair + 1) * GLA_DK)
            cols = slice(2 * pair * GLA_DV, 2 * (pair + 1) * GLA_DV)
            q_cat = jnp.concatenate([qd_ref[rows, kdims],
                                     qd_ref[rows, KW + 2 * pair * GLA_DK:KW + 2 * (pair + 1) * GLA_DK]],
                                    axis=1)
            s_cat = jnp.concatenate([pair_operand(upd_ref[n, 0, kdims, :]),
                                     pair_operand(upd_ref[n, 1, kdims, :])], axis=0)
            o_p = oacc_ref[rows, cols] + _dot(q_cat, s_cat)
            ms = _dot((o_p * o_p).astype(BF16), pair_mean)
            y_ref[rows, cols] = (o_p * lax.rsqrt(ms + RMS_EPS) * gam_ref[:, cols]).astype(y_ref.dtype)
        return carry

    lax.fori_loop(0, n_chunks, cross_chunk_and_norm, 0, unroll=GLA_UNROLL)


def _gla(q, k, v, g, wg, bg, gamma, batch):
    tok = lambda b: (b, 0)
    const = lambda b: (0, 0)
    n_chunks = SEQ // GLA_CHUNK
    return pl.pallas_call(
        _gla_kernel,
        grid=(batch,),
        in_specs=[pl.BlockSpec((SEQ, GLA_KEY_WIDTH), tok),
                  pl.BlockSpec((SEQ, GLA_KEY_WIDTH), tok),
                  pl.BlockSpec((SEQ, GLA_WIDTH), tok),
                  pl.BlockSpec((SEQ, GATE_PAD), tok),
                  pl.BlockSpec((GATE_PAD, 2 * GLA_KEY_WIDTH), const),
                  pl.BlockSpec((1, 2 * GLA_KEY_WIDTH), const),
                  pl.BlockSpec((1, GLA_WIDTH), const)],
        out_specs=pl.BlockSpec((SEQ, GLA_WIDTH), tok),
        out_shape=jax.ShapeDtypeStruct((batch * SEQ, GLA_WIDTH), BF16),
        scratch_shapes=[pltpu.VMEM((SEQ, GLA_WIDTH), F32),
                        pltpu.VMEM((SEQ, 2 * GLA_KEY_WIDTH), BF16),
                        pltpu.VMEM((SEQ, 2 * GLA_KEY_WIDTH), BF16),
                        pltpu.VMEM((n_chunks, 2 * GLA_KEY_WIDTH, GLA_CHUNK), BF16),
                        pltpu.VMEM((n_chunks, 2, GLA_KEY_WIDTH, GLA_DV), F32),
                        pltpu.VMEM((n_chunks, 2, SUBLANES, GLA_KEY_WIDTH), F32),
                        pltpu.VMEM((n_chunks, 2, GLA_KEY_WIDTH, LANES), F32),
                        pltpu.SMEM((n_chunks,), jnp.int32),
                        pltpu.VMEM((GLA_CHUNK, 2 * GLA_KEY_WIDTH), F32),
                        pltpu.VMEM((GLA_CHUNK, GLA_KEY_WIDTH), F32),
                        pltpu.VMEM((GLA_HEADS, GLA_CHUNK, GLA_CHUNK), F32)],
        compiler_params=pltpu.CompilerParams(dimension_semantics=("arbitrary",),
                                             vmem_limit_bytes=VMEM_LIMIT_BYTES),
        name="gla",
    )(q, k, v, g, wg, bg, gamma)


def _fnet_tables():
    R = RADIX
    n = SEQ
    a = np.arange(R)
    lo8 = np.arange(SUBLANES)
    eye8 = np.eye(SUBLANES)

    ka = (8 * np.arange(2)[:, None] + lo8[None, :])
    ang = 2 * np.pi * (256 * a[None, None, :] * ka[:, :, None]) / n
    eye16 = np.eye(R)
    def s1(trig):
        t = trig(ang)
        m = t[:, None, :, :, None] * eye16[None, :, None, None, :]
        return m.reshape(2 * R * SUBLANES, R * R)
    m1 = np.concatenate([s1(np.cos), -s1(np.sin)], axis=0)

    def complex_block(theta):
        c = np.cos(theta)
        s = -np.sin(theta)
        def kron(t):
            m = t[:, :, :, None] * eye8[None, :, None, :]
            return m.reshape(t.shape[0] * SUBLANES, t.shape[2] * SUBLANES)
        mr, mi = kron(c), kron(s)
        return np.block([[mr, -mi], [mi, mr]])

    m2 = []
    for kah in range(2):
        k_a = 8 * kah + lo8
        expo = 16 * a[None, None, :] * k_a[None, :, None] + 256 * a[None, None, :] * a[:, None, None]
        m2.append(complex_block(2 * np.pi * expo / n))
    m2 = np.stack(m2)

    m3 = np.zeros((2, R, 2 * R * SUBLANES, 2 * R * SUBLANES))
    for kah in range(2):
        k_a = 8 * kah + lo8
        for kb in range(R):
            expo = (a[None, None, :] * k_a[None, :, None] + 16 * a[None, None, :] * kb
                    + 256 * a[None, None, :] * a[:, None, None])
            m3[kah, kb] = complex_block(2 * np.pi * expo / n)

    c = np.arange(FNET_GDIM)
    ang_c = 2 * np.pi * np.outer(c, c) / FNET_GDIM
    norm = 1.0 / math.sqrt(SEQ * FNET_GDIM)
    chan = np.concatenate([np.cos(ang_c), np.sin(ang_c)], axis=0) * norm
    m1, m2, m3 = lax.optimization_barrier((jnp.asarray(m1, F32), jnp.asarray(m2, F32),
                                           jnp.asarray(m3, F32)))
    return m1.astype(BF16), m2.astype(BF16), m3.astype(BF16), jnp.asarray(chan, F32)


def _fnet_kernel(u_ref, zf_ref, wf_ref, m1_ref, m2_ref, m3_ref, chan_ref, out_ref,
                 y1r, y1i, y2r, y2i, mg_ref):
    R = RADIX
    T = SUBLANES
    W = FNET_HALF

    @pl.when(pl.program_id(0) == 0)
    def _():
        for g in range(FNET_GROUPS):
            mg_ref[g] = _dot_f32(chan_ref[...], wf_ref[g]).astype(BF16)

    for half in range(FNET_WIDTH // W):
        lanes = slice(half * W, (half + 1) * W)

        def stage1(sm, carry):
            blk = u_ref[:, sm, :, lanes].reshape(R * R, W)
            res = _dot(m1_ref[...], blk)
            y1r[:, :, sm, :, :] = res[:R * R].reshape(2, R, T, W)
            y1i[:, :, sm, :, :] = res[R * R:].reshape(2, R, T, W)
            return carry

        lax.fori_loop(0, R, stage1, 0, unroll=FNET_UNROLL)

        def stage2(i, carry):
            kah = i // R
            sl = i % R
            rhs = jnp.concatenate([y1r[kah, sl].reshape(R * T, W), y1i[kah, sl].reshape(R * T, W)],
                                  axis=0).astype(BF16)
            res = _dot(m2_ref[kah], rhs)
            y2r[kah, :, sl, :, :] = res[:R * T].reshape(R, T, W)
            y2i[kah, :, sl, :, :] = res[R * T:].reshape(R, T, W)
            return carry

        lax.fori_loop(0, 2 * R, stage2, 0, unroll=FNET_UNROLL)

        def stage3(kb, carry):
            mixed = []
            for kah in range(2):
                rhs = jnp.concatenate([y2r[kah, kb].reshape(R * T, W), y2i[kah, kb].reshape(R * T, W)],
                                      axis=0).astype(BF16)
                res = _dot(m3_ref[kah, kb], rhs)
                groups = []
                for j in range(W // FNET_GDIM):
                    gl = slice(j * FNET_GDIM, (j + 1) * FNET_GDIM)
                    z = jnp.concatenate([res[:R * T, gl], res[R * T:, gl]], axis=1).astype(BF16)
                    groups.append(_dot(z, mg_ref[half * (W // FNET_GDIM) + j]))
                mixed.append(jnp.concatenate(groups, axis=1).reshape(R, T, W))
            uf = jnp.stack(mixed, axis=1).reshape(R, 2 * T, W)
            gate = _silu(zf_ref[:, kb, :, lanes].astype(F32))
            out_ref[:, kb, :, lanes] = (uf * gate).astype(out_ref.dtype)
            return carry

        lax.fori_loop(0, R, stage3, 0, unroll=FNET_UNROLL)


def _fnet(u5, zf5, w_fnet, tables, batch):
    m1, m2, m3, chan = tables
    R, T, W = RADIX, SUBLANES, FNET_HALF
    return pl.pallas_call(
        _fnet_kernel,
        grid=(batch,),
        in_specs=[pl.BlockSpec((None, R, R, R, FNET_WIDTH), lambda b: (b, 0, 0, 0, 0)),
                  pl.BlockSpec((None, R, R, R, FNET_WIDTH), lambda b: (b, 0, 0, 0, 0)),
                  pl.BlockSpec((FNET_GROUPS, FNET_GDIM, FNET_GDIM), lambda b: (0, 0, 0)),
                  pl.BlockSpec(m1.shape, lambda b: (0, 0)),
                  pl.BlockSpec(m2.shape, lambda b: (0, 0, 0)),
                  pl.BlockSpec(m3.shape, lambda b: (0, 0, 0, 0)),
                  pl.BlockSpec(chan.shape, lambda b: (0, 0))],
        out_specs=pl.BlockSpec((None, R, R, R, FNET_WIDTH), lambda b: (b, 0, 0, 0, 0)),
        out_shape=jax.ShapeDtypeStruct((batch, R, R, R, FNET_WIDTH), BF16),
        scratch_shapes=[pltpu.VMEM((2, R, R, T, W), F32), pltpu.VMEM((2, R, R, T, W), F32),
                        pltpu.VMEM((2, R, R, T, W), F32), pltpu.VMEM((2, R, R, T, W), F32),
                        pltpu.VMEM((FNET_GROUPS, 2 * FNET_GDIM, FNET_GDIM), BF16)],
        compiler_params=pltpu.CompilerParams(dimension_semantics=("arbitrary",),
                                             vmem_limit_bytes=VMEM_LIMIT_BYTES),
        name="fnet",
    )(u5, zf5, w_fnet, m1, m2, m3, chan)


def _out_kernel(x_ref, og_ref, zg_ref, yf_ref, w_ref, lng_ref, lnb_ref, o_ref):
    sub = OUT_ROW_TILE // OUT_SUBTILES
    ys = []
    for t in range(OUT_SUBTILES):
        rows = slice(t * sub, (t + 1) * sub)
        y_gla = (og_ref[rows, :].astype(F32) * _silu(zg_ref[rows, :].astype(F32))).astype(BF16)
        ys.append(_dot(jnp.concatenate([y_gla, yf_ref[rows, :]], axis=1), w_ref[...]))
    for t in range(OUT_SUBTILES):
        rows = slice(t * sub, (t + 1) * sub)
        r = DEEPNORM_ALPHA * x_ref[rows, :] + ys[t]
        mu = jnp.mean(r, axis=-1, keepdims=True)
        d = r - mu
        var = jnp.mean(d * d, axis=-1, keepdims=True)
        o_ref[rows, :] = d * lax.rsqrt(var + LN_EPS) * lng_ref[...] + lnb_ref[...]


def _out_proj(x2d, o_gla, zg, y_fnet, w_out, ln_g, ln_b):
    m = x2d.shape[0]
    row = lambda i: (i, 0)
    const = lambda i: (0, 0)
    return pl.pallas_call(
        _out_kernel,
        grid=(m // OUT_ROW_TILE,),
        in_specs=[pl.BlockSpec((OUT_ROW_TILE, D_MODEL), row),
                  pl.BlockSpec((OUT_ROW_TILE, GLA_WIDTH), row),
                  pl.BlockSpec((OUT_ROW_TILE, GLA_WIDTH), row),
                  pl.BlockSpec((OUT_ROW_TILE, FNET_WIDTH), row),
                  pl.BlockSpec((GLA_WIDTH + FNET_WIDTH, D_MODEL), const),
                  pl.BlockSpec((1, D_MODEL), const),
                  pl.BlockSpec((1, D_MODEL), const)],
        out_specs=pl.BlockSpec((OUT_ROW_TILE, D_MODEL), row),
        out_shape=jax.ShapeDtypeStruct((m, D_MODEL), F32),
        compiler_params=pltpu.CompilerParams(dimension_semantics=("arbitrary",),
                                             vmem_limit_bytes=VMEM_LIMIT_BYTES),
        name="out_proj",
    )(x2d, o_gla, zg, y_fnet, w_out, ln_g, ln_b)


def _permute_w_in(w):
    kw, gw, fw, r = GLA_KEY_WIDTH, GLA_WIDTH, FNET_WIDTH, GATE_RANK
    o_q, o_k, o_v = 0, kw, 2 * kw
    o_gf = o_v + gw
    o_zg = o_gf + 2 * r
    o_u = o_zg + gw
    o_zf = o_u + fw
    gates = jnp.concatenate([w[:, o_gf:o_gf + 2 * r]] * (GATE_PAD // (2 * r)), axis=1)
    cols = [w[:, o_q:o_q + kw] * (GLA_DK ** -0.5), w[:, o_k:o_k + kw], w[:, o_v:o_v + gw], w[:, o_zg:o_zg + gw],
            w[:, o_u:o_u + fw], w[:, o_zf:o_zf + fw], gates]
    return jnp.concatenate(cols, axis=1).astype(BF16)


def kernel(x, w_in, w_gate_up_fwd, b_gate_fwd, w_gate_up_bwd, b_gate_bwd, gla_norm_g,
           w_fnet, w_out, ln_g, ln_b):
    batch, seq, d = x.shape
    assert (seq, d) == (SEQ, D_MODEL) and w_in.shape[0] == DEPTH
    tables = _fnet_tables()
    for l in range(DEPTH):
        x2d = x.reshape(batch * seq, d)
        q, k, v, zg, u, zf, g = _projection(x2d, _permute_w_in(w_in[l]))

        zero = jnp.zeros((GATE_RANK, GLA_KEY_WIDTH), F32)
        w_bd = jnp.block([[w_gate_up_fwd[l], zero], [zero, w_gate_up_bwd[l]]]) * LOG2_E
        w_hi = w_bd.astype(BF16)
        w_lo = (w_bd - w_hi.astype(F32)).astype(BF16)
        wg = jnp.concatenate([w_hi, w_hi, w_lo, jnp.zeros_like(w_hi)], axis=0)
        bg = jnp.concatenate([b_gate_fwd[l], b_gate_bwd[l]])[None, :] * LOG2_E
        o_gla = _gla(q, k, v, g, wg, bg, gla_norm_g[l][None, :], batch)

        seq_digits = (batch, RADIX, RADIX, RADIX, FNET_WIDTH)
        y_fnet = _fnet(u.reshape(seq_digits), zf.reshape(seq_digits), w_fnet[l], tables, batch)

        out = _out_proj(x2d, o_gla, zg, y_fnet.reshape(batch * seq, FNET_WIDTH),
                        w_out[l].astype(BF16), ln_g[l][None, :], ln_b[l][None, :])
        x = out.reshape(batch, seq, d)
    return x
```

```python
import functools
import math

import numpy as np
import jax
import jax.numpy as jnp
from jax import lax
from jax.experimental import pallas as pl
from jax.experimental.pallas import tpu as pltpu

F32 = jnp.float32
BF16 = jnp.bfloat16

D_MODEL = 1024
SEQ = 4096
GLA_HEADS = 4
GLA_DK = 64
GLA_DV = 128
GLA_KEY_WIDTH = GLA_HEADS * GLA_DK
GLA_WIDTH = GLA_HEADS * GLA_DV
GATE_RANK = 16
GATE_TAU = 16.0
FNET_GROUPS = 4
FNET_GDIM = 128
FNET_WIDTH = FNET_GROUPS * FNET_GDIM
LN_EPS = 1e-5
RMS_EPS = 1e-6
DEPTH = 1
DEEPNORM_ALPHA = (2.0 * DEPTH) ** 0.25
LOG2_E = 1.0 / math.log(2.0)

LANES = 128
SUBLANES = 8
VMEM_LIMIT_BYTES = 60 * 1024 * 1024

N_PARTS = 2
ROW_TILE = 1024
OUT_SUBTILES = 4
GLA_CHUNK = 256
GATE_PAD = LANES
RADIX = 16
FNET_HALF = 256
GLA_SAFE_LOG2_DECAY = 100.0
GLA_PIPE_UNROLL = 15
GLA_UNROLL = 8
FNET_UNROLL = 32

_PROJ_COLS = (("q", GLA_KEY_WIDTH), ("k", GLA_KEY_WIDTH), ("v", GLA_WIDTH), ("zg", GLA_WIDTH),
              ("u", FNET_WIDTH), ("zf", FNET_WIDTH), ("g", GATE_PAD))
PROJ_WIDTH = sum(w for _, w in _PROJ_COLS)


def _dot(a, b):
    return jnp.dot(a, b, preferred_element_type=F32)


def _split2(x):
    hi = x.astype(BF16)
    lo = (x - hi.astype(F32)).astype(BF16)
    return hi, lo


def _split3(x):
    hi = x.astype(BF16)
    r = x - hi.astype(F32)
    mid = r.astype(BF16)
    lo = (r - mid.astype(F32)).astype(BF16)
    return hi, mid, lo


def _dot_f32(a, b):
    ah, al = _split2(a)
    bh, bl = _split2(b)
    return _dot(ah, bh) + _dot(ah, bl) + _dot(al, bh)


def _silu(z):
    h = 0.5 * z
    return h + h * jnp.tanh(h)


def _proj_kernel(x_ref, w_ref, q_ref, k_ref, v_ref, zg_ref, u_ref, zf_ref, g_ref):
    xb = x_ref[...].astype(BF16)
    outs = (q_ref, k_ref, v_ref, zg_ref, u_ref, zf_ref, g_ref)
    off = 0
    for (name, width), o_ref in zip(_PROJ_COLS, outs):
        r = _dot(xb, w_ref[:, off:off + width])
        if name == "g":
            hi, lo = _split2(r)
            lane = lax.broadcasted_iota(jnp.int32, r.shape, 1)
            r = jnp.where((lane >= 2 * GATE_RANK) & (lane < 4 * GATE_RANK), lo, hi)
        o_ref[...] = r.astype(o_ref.dtype)
        off += width


def _gla_kernel(q_ref, k_ref, v_ref, g_ref, wg_ref, bg_ref, gam_ref, y_ref,
                oacc_ref, qd_ref, ki_ref, kdt_ref, upd_ref, dec_ref, dect_ref,
                flag_ref, bsc_ref, qsc_ref, psc_ref):
    C = GLA_CHUNK
    KW = GLA_KEY_WIDTH
    n_chunks = SEQ // C
    row_i = lax.broadcasted_iota(jnp.int32, (C, C), 0)
    col_i = lax.broadcasted_iota(jnp.int32, (C, C), 1)
    lower = row_i >= col_i
    upper = row_i <= col_i
    lane_head = lax.broadcasted_iota(jnp.int32, (C, KW), 1) // GLA_DK
    pair_top = lax.broadcasted_iota(jnp.int32, (2 * GLA_DK, GLA_DV), 0) < GLA_DK
    same_head = (lax.broadcasted_iota(jnp.int32, (2 * GLA_DV, 2 * GLA_DV), 0) // GLA_DV ==
                 lax.broadcasted_iota(jnp.int32, (2 * GLA_DV, 2 * GLA_DV), 1) // GLA_DV)
    pair_mean = jnp.where(same_head, 1.0 / GLA_DV, 0.0).astype(BF16)

    def chunk_rows(n):
        return pl.ds(pl.multiple_of(n * C, C), C)

    def gate_preact(rows):
        return _dot(g_ref[rows, :], wg_ref[...]) + bg_ref[...]

    def gate_logs(z2):
        tau_log2_a = jnp.minimum(z2, 0.0) - jnp.log2(1.0 + jnp.exp2(-jnp.abs(z2)))
        hi = tau_log2_a.astype(BF16)
        mid = (tau_log2_a - hi.astype(F32)).astype(BF16)
        return hi, mid

    def cum_log_decays(hi, mid):
        lo_t = jnp.where(lower, 1.0 / GATE_TAU, 0.0).astype(BF16)
        up_t = jnp.where(upper, 1.0 / GATE_TAU, 0.0).astype(BF16)
        b_f = _dot(jnp.concatenate([lo_t, lo_t], axis=1),
                   jnp.concatenate([hi[:, :KW], mid[:, :KW]], axis=0))
        b_b = _dot(jnp.concatenate([up_t, up_t], axis=1),
                   jnp.concatenate([hi[:, KW:], mid[:, KW:]], axis=0))
        return b_f, b_b

    def scale_qk(n, b_f, b_b):
        rows = chunk_rows(n)
        e_f = b_f[C - 1:C, :]
        e_b = b_b[0:1, :]
        dec_ref[n, 0] = jnp.broadcast_to(jnp.exp2(e_f), (SUBLANES, KW))
        dec_ref[n, 1] = jnp.broadcast_to(jnp.exp2(e_b), (SUBLANES, KW))
        flag_ref[n] = (jnp.max(-jnp.minimum(e_f, e_b)) > GLA_SAFE_LOG2_DECAY).astype(jnp.int32)

        qf = q_ref[rows, :].astype(F32)
        kf = k_ref[rows, :].astype(F32)
        qd_ref[rows, :KW] = (qf * jnp.exp2(b_f)).astype(BF16)
        qd_ref[rows, KW:] = (qf * jnp.exp2(b_b)).astype(BF16)
        ki_ref[rows, :KW] = (kf * jnp.exp2(-b_f)).astype(BF16)
        ki_ref[rows, KW:] = (kf * jnp.exp2(-b_b)).astype(BF16)
        kd = jnp.concatenate([kf * jnp.exp2(e_f - b_f), kf * jnp.exp2(e_b - b_b)], axis=1)
        kdt_ref[n] = jnp.transpose(kd).astype(BF16)

    def scores(n):
        rows = chunk_rows(n)
        nt = (((1,), (1,)), ((), ()))
        qd_f = qd_ref[rows, :KW]
        qd_b = qd_ref[rows, KW:]
        probs = []
        for h in range(GLA_HEADS):
            in_head = lane_head == h
            s_f = lax.dot_general(jnp.where(in_head, qd_f, jnp.zeros_like(qd_f)), ki_ref[rows, :KW],
                                  nt, preferred_element_type=F32)
            s_b = lax.dot_general(jnp.where(in_head, qd_b, jnp.zeros_like(qd_b)), ki_ref[rows, KW:],
                                  nt, preferred_element_type=F32)
            probs.append((jnp.where(lower, s_f, 0.0) + jnp.where(upper, s_b, 0.0)).astype(BF16))
        return probs

    def weigh_values(n, probs):
        rows = chunk_rows(n)
        for h in range(GLA_HEADS):
            vcols = slice(h * GLA_DV, (h + 1) * GLA_DV)
            lhs = jnp.concatenate([probs[h], kdt_ref[n, h * GLA_DK:(h + 1) * GLA_DK, :],
                                   kdt_ref[n, KW + h * GLA_DK:KW + (h + 1) * GLA_DK, :]], axis=0)
            r = _dot(lhs, v_ref[rows, vcols])
            oacc_ref[rows, vcols] = r[:C]
            upd_ref[n, 0, h * GLA_DK:(h + 1) * GLA_DK, :] = r[C:C + GLA_DK]
            upd_ref[n, 1, h * GLA_DK:(h + 1) * GLA_DK, :] = r[C + GLA_DK:]
        for d in range(2):
            dect_ref[n, d] = jnp.transpose(jnp.broadcast_to(dec_ref[n, d][0:1, :], (LANES, KW)))

    def attend_and_scale_next(n, carry):
        z2 = gate_preact(chunk_rows(n + 1))
        probs = scores(n)
        b_next = cum_log_decays(*gate_logs(z2))
        weigh_values(n, probs)
        scale_qk(n + 1, *b_next)
        return carry

    scale_qk(0, *cum_log_decays(*gate_logs(gate_preact(chunk_rows(0)))))
    lax.fori_loop(0, n_chunks - 1, attend_and_scale_next, 0, unroll=GLA_PIPE_UNROLL)
    weigh_values(n_chunks - 1, scores(n_chunks - 1))

    def redo_exact(n, carry):
        @pl.when(flag_ref[n] != 0)
        def _():
            rows = pl.ds(pl.multiple_of(n * C, C), C)
            nt = (((1,), (1,)), ((), ()))
            b_f, b_b = cum_log_decays(*gate_logs(gate_preact(rows)))
            bsc_ref[:, :KW] = b_f
            bsc_ref[:, KW:] = b_b
            qsc_ref[...] = q_ref[rows, :].astype(F32)
            col = lax.broadcasted_iota(jnp.int32, (SUBLANES, C), 1)
            head_row = (lax.broadcasted_iota(jnp.int32, (SUBLANES, KW), 0) ==
                        lax.broadcasted_iota(jnp.int32, (SUBLANES, KW), 1) // GLA_DK)

            def row_scores(i, c):
                kf = k_ref[rows, :].astype(F32)
                q_heads = jnp.where(head_row, jnp.broadcast_to(qsc_ref[pl.ds(i, 1), :], (SUBLANES, KW)),
                                    0.0).astype(BF16)
                w_f = jnp.exp2(jnp.minimum(bsc_ref[pl.ds(i, 1), :KW] - bsc_ref[:, :KW], 0.0))
                w_b = jnp.exp2(jnp.minimum(bsc_ref[pl.ds(i, 1), KW:] - bsc_ref[:, KW:], 0.0))
                s_f = lax.dot_general(q_heads, (kf * w_f).astype(BF16), nt, preferred_element_type=F32)
                s_b = lax.dot_general(q_heads, (kf * w_b).astype(BF16), nt, preferred_element_type=F32)
                p = jnp.where(col <= i, s_f, 0.0) + jnp.where(col >= i, s_b, 0.0)
                for h in range(GLA_HEADS):
                    psc_ref[h, pl.ds(i, 1), :] = p[h:h + 1, :]
                return c

            lax.fori_loop(0, C, row_scores, 0)
            for h in range(GLA_HEADS):
                vcols = slice(h * GLA_DV, (h + 1) * GLA_DV)
                oacc_ref[rows, vcols] = _dot(psc_ref[h].astype(BF16), v_ref[rows, vcols])
        return carry

    lax.fori_loop(0, n_chunks, redo_exact, 0)

    def enter_state(n, d, state):
        increment = upd_ref[n, d]
        upd_ref[n, d] = state
        return state * dect_ref[n, d] + increment

    def scan_states(j, states):
        s_f, s_b = states
        return enter_state(j, 0, s_f), enter_state(n_chunks - 1 - j, 1, s_b)

    state0 = jnp.zeros((KW, GLA_DV), F32)
    lax.fori_loop(0, n_chunks, scan_states, (state0, state0), unroll=GLA_UNROLL)

    def pair_operand(state_rows):
        sb = state_rows.astype(BF16)
        zero = jnp.zeros_like(sb)
        return jnp.concatenate([jnp.where(pair_top, sb, zero), jnp.where(pair_top, zero, sb)], axis=1)

    def cross_chunk_and_norm(n, carry):
        rows = chunk_rows(n)
        for pair in range(GLA_HEADS // 2):
            kdims = slice(2 * pair * GLA_DK, 2 * (pair + 1) * GLA_DK)
            cols = slice(2 * pair * GLA_DV, 2 * (pair + 1) * GLA_DV)
            q_cat = jnp.concatenate([qd_ref[rows, kdims],
                                     qd_ref[rows, KW + 2 * pair * GLA_DK:KW + 2 * (pair + 1) * GLA_DK]],
                                    axis=1)
            s_cat = jnp.concatenate([pair_operand(upd_ref[n, 0, kdims, :]),
                                     pair_operand(upd_ref[n, 1, kdims, :])], axis=0)
            o_p = oacc_ref[rows, cols] + _dot(q_cat, s_cat)
            ms = _dot((o_p * o_p).astype(BF16), pair_mean)
            y_ref[rows, cols] = (o_p * lax.rsqrt(ms + RMS_EPS) * gam_ref[:, cols]).astype(y_ref.dtype)
        return carry

    lax.fori_loop(0, n_chunks, cross_chunk_and_norm, 0, unroll=GLA_UNROLL)


def _gla(q, k, v, g, wg, bg, gamma, batch):
    tok = lambda b: (b, 0)
    const = lambda b: (0, 0)
    n_chunks = SEQ // GLA_CHUNK
    return pl.pallas_call(
        _gla_kernel,
        grid=(batch,),
        in_specs=[pl.BlockSpec((SEQ, GLA_KEY_WIDTH), tok),
                  pl.BlockSpec((SEQ, GLA_KEY_WIDTH), tok),
                  pl.BlockSpec((SEQ, GLA_WIDTH), tok),
                  pl.BlockSpec((SEQ, GATE_PAD), tok),
                  pl.BlockSpec((GATE_PAD, 2 * GLA_KEY_WIDTH), const),
                  pl.BlockSpec((1, 2 * GLA_KEY_WIDTH), const),
                  pl.BlockSpec((1, GLA_WIDTH), const)],
        out_specs=pl.BlockSpec((SEQ, GLA_WIDTH), tok),
        out_shape=jax.ShapeDtypeStruct((batch * SEQ, GLA_WIDTH), BF16),
        scratch_shapes=[pltpu.VMEM((SEQ, GLA_WIDTH), F32),
                        pltpu.VMEM((SEQ, 2 * GLA_KEY_WIDTH), BF16),
                        pltpu.VMEM((SEQ, 2 * GLA_KEY_WIDTH), BF16),
                        pltpu.VMEM((n_chunks, 2 * GLA_KEY_WIDTH, GLA_CHUNK), BF16),
                        pltpu.VMEM((n_chunks, 2, GLA_KEY_WIDTH, GLA_DV), F32),
                        pltpu.VMEM((n_chunks, 2, SUBLANES, GLA_KEY_WIDTH), F32),
                        pltpu.VMEM((n_chunks, 2, GLA_KEY_WIDTH, LANES), F32),
                        pltpu.SMEM((n_chunks,), jnp.int32),
                        pltpu.VMEM((GLA_CHUNK, 2 * GLA_KEY_WIDTH), F32),
                        pltpu.VMEM((GLA_CHUNK, GLA_KEY_WIDTH), F32),
                        pltpu.VMEM((GLA_HEADS, GLA_CHUNK, GLA_CHUNK), F32)],
        compiler_params=pltpu.CompilerParams(dimension_semantics=("arbitrary",),
                                             vmem_limit_bytes=VMEM_LIMIT_BYTES),
        name="gla",
    )(q, k, v, g, wg, bg, gamma)


def _fnet_tables():
    R = RADIX
    n = SEQ
    a = np.arange(R)
    lo8 = np.arange(SUBLANES)
    eye8 = np.eye(SUBLANES)

    ka = (8 * np.arange(2)[:, None] + lo8[None, :])
    ang = 2 * np.pi * (256 * a[None, None, :] * ka[:, :, None]) / n
    eye16 = np.eye(R)
    def s1(trig):
        t = trig(ang)
        m = t[:, None, :, :, None] * eye16[None, :, None, None, :]
        return m.reshape(2 * R * SUBLANES, R * R)
    m1 = np.concatenate([s1(np.cos), -s1(np.sin)], axis=0)

    def complex_block(theta):
        c = np.cos(theta)
        s = -np.sin(theta)
        def kron(t):
            m = t[:, :, :, None] * eye8[None, :, None, :]
            return m.reshape(t.shape[0] * SUBLANES, t.shape[2] * SUBLANES)
        mr, mi = kron(c), kron(s)
        return np.block([[mr, -mi], [mi, mr]])

    m2 = []
    for kah in range(2):
        k_a = 8 * kah + lo8
        expo = 16 * a[None, None, :] * k_a[None, :, None] + 256 * a[None, None, :] * a[:, None, None]
        m2.append(complex_block(2 * np.pi * expo / n))
    m2 = np.stack(m2)

    m3 = np.zeros((2, R, 2 * R * SUBLANES, 2 * R * SUBLANES))
    for kah in range(2):
        k_a = 8 * kah + lo8
        for kb in range(R):
            expo = (a[None, None, :] * k_a[None, :, None] + 16 * a[None, None, :] * kb
                    + 256 * a[None, None, :] * a[:, None, None])
            m3[kah, kb] = complex_block(2 * np.pi * expo / n)

    c = np.arange(FNET_GDIM)
    ang_c = 2 * np.pi * np.outer(c, c) / FNET_GDIM
    norm = 1.0 / math.sqrt(SEQ * FNET_GDIM)
    chan = np.concatenate([np.cos(ang_c), np.sin(ang_c)], axis=0) * norm
    m1, m2, m3 = lax.optimization_barrier((jnp.asarray(m1, F32), jnp.asarray(m2, F32),
                                           jnp.asarray(m3, F32)))
    return m1.astype(BF16), m2.astype(BF16), m3.astype(BF16), jnp.asarray(chan, F32)


def _fnet_kernel(u_ref, zf_ref, wf_ref, m1_ref, m2_ref, m3_ref, chan_ref, out_ref,
                 y1r, y1i, y2r, y2i, mg_ref):
    R = RADIX
    T = SUBLANES
    W = FNET_HALF

    @pl.when(pl.program_id(0) == 0)
    def _():
        for g in range(FNET_GROUPS):
            mg_ref[g] = _dot_f32(chan_ref[...], wf_ref[g]).astype(BF16)

    for half in range(FNET_WIDTH // W):
        lanes = slice(half * W, (half + 1) * W)

        def stage1(sm, carry):
            blk = u_ref[:, sm, :, lanes].reshape(R * R, W)
            res = _dot(m1_ref[...], blk)
            y1r[:, :, sm, :, :] = res[:R * R].reshape(2, R, T, W)
            y1i[:, :, sm, :, :] = res[R * R:].reshape(2, R, T, W)
            return carry

        lax.fori_loop(0, R, stage1, 0, unroll=FNET_UNROLL)

        def stage2(i, carry):
            kah = i // R
            sl = i % R
            rhs = jnp.concatenate([y1r[kah, sl].reshape(R * T, W), y1i[kah, sl].reshape(R * T, W)],
                                  axis=0).astype(BF16)
            res = _dot(m2_ref[kah], rhs)
            y2r[kah, :, sl, :, :] = res[:R * T].reshape(R, T, W)
            y2i[kah, :, sl, :, :] = res[R * T:].reshape(R, T, W)
            return carry

        lax.fori_loop(0, 2 * R, stage2, 0, unroll=FNET_UNROLL)

        def stage3(kb, carry):
            mixed = []
            for kah in range(2):
                rhs = jnp.concatenate([y2r[kah, kb].reshape(R * T, W), y2i[kah, kb].reshape(R * T, W)],
                                      axis=0).astype(BF16)
                res = _dot(m3_ref[kah, kb], rhs)
                groups = []
                for j in range(W // FNET_GDIM):
                    gl = slice(j * FNET_GDIM, (j + 1) * FNET_GDIM)
                    z = jnp.concatenate([res[:R * T, gl], res[R * T:, gl]], axis=1).astype(BF16)
                    groups.append(_dot(z, mg_ref[half * (W // FNET_GDIM) + j]))
                mixed.append(jnp.concatenate(groups, axis=1).reshape(R, T, W))
            uf = jnp.stack(mixed, axis=1).reshape(R, 2 * T, W)
            gate = _silu(zf_ref[:, kb, :, lanes].astype(F32))
            out_ref[:, kb, :, lanes] = (uf * gate).astype(out_ref.dtype)
            return carry

        lax.fori_loop(0, R, stage3, 0, unroll=FNET_UNROLL)


def _fnet(u5, zf5, w_fnet, tables, batch):
    m1, m2, m3, chan = tables
    R, T, W = RADIX, SUBLANES, FNET_HALF
    return pl.pallas_call(
        _fnet_kernel,
        grid=(batch,),
        in_specs=[pl.BlockSpec((None, R, R, R, FNET_WIDTH), lambda b: (b, 0, 0, 0, 0)),
                  pl.BlockSpec((None, R, R, R, FNET_WIDTH), lambda b: (b, 0, 0, 0, 0)),
                  pl.BlockSpec((FNET_GROUPS, FNET_GDIM, FNET_GDIM), lambda b: (0, 0, 0)),
                  pl.BlockSpec(m1.shape, lambda b: (0, 0)),
                  pl.BlockSpec(m2.shape, lambda b: (0, 0, 0)),
                  pl.BlockSpec(m3.shape, lambda b: (0, 0, 0, 0)),
                  pl.BlockSpec(chan.shape, lambda b: (0, 0))],
        out_specs=pl.BlockSpec((None, R, R, R, FNET_WIDTH), lambda b: (b, 0, 0, 0, 0)),
        out_shape=jax.ShapeDtypeStruct((batch, R, R, R, FNET_WIDTH), BF16),
        scratch_shapes=[pltpu.VMEM((2, R, R, T, W), F32), pltpu.VMEM((2, R, R, T, W), F32),
                        pltpu.VMEM((2, R, R, T, W), F32), pltpu.VMEM((2, R, R, T, W), F32),
                        pltpu.VMEM((FNET_GROUPS, 2 * FNET_GDIM, FNET_GDIM), BF16)],
        compiler_params=pltpu.CompilerParams(dimension_semantics=("arbitrary",),
                                             vmem_limit_bytes=VMEM_LIMIT_BYTES),
        name="fnet",
    )(u5, zf5, w_fnet, m1, m2, m3, chan)


def _out_kernel(x_ref, og_ref, zg_ref, yf_ref, w_ref, lng_ref, lnb_ref, o_ref):
    sub = x_ref.shape[0] // OUT_SUBTILES
    ys = []
    for t in range(OUT_SUBTILES):
        rows = slice(t * sub, (t + 1) * sub)
        y_gla = (og_ref[rows, :].astype(F32) * _silu(zg_ref[rows, :].astype(F32))).astype(BF16)
        ys.append(_dot(jnp.concatenate([y_gla, yf_ref[rows, :]], axis=1), w_ref[...]))
    for t in range(OUT_SUBTILES):
        rows = slice(t * sub, (t + 1) * sub)
        r = DEEPNORM_ALPHA * x_ref[rows, :] + ys[t]
        mu = jnp.mean(r, axis=-1, keepdims=True)
        d = r - mu
        var = jnp.mean(d * d, axis=-1, keepdims=True)
        o_ref[rows, :] = d * lax.rsqrt(var + LN_EPS) * lng_ref[...] + lnb_ref[...]


N_OUT_INPUTS = 7


def _proj_stage_kernel(*refs, proj, out, carry_out):
    n_proj_in = 2 if proj else 0
    n_in = n_proj_in + (N_OUT_INPUTS if out else 0) + (1 if carry_out else 0)
    ins, outs = refs[:n_in], refs[n_in:]
    if out:
        _out_kernel(*ins[n_proj_in:n_proj_in + N_OUT_INPUTS], outs[-1])
    if proj:
        _proj_kernel(*ins[:n_proj_in], *outs[:len(_PROJ_COLS)])


def _proj_stage(x2d, proj_part=None, w_perm=None, out_part=None, out_args=None, out_prev=None):
    m = x2d.shape[0]
    rows_part = m // N_PARTS
    steps = rows_part // ROW_TILE
    row = lambda i: (i, 0)
    const = lambda i: (0, 0)
    args, in_specs, out_shape, out_specs, aliases = [], [], [], [], {}
    if proj_part is not None:
        first = proj_part * steps
        args += [x2d, w_perm]
        in_specs += [pl.BlockSpec((ROW_TILE, D_MODEL), lambda i: (i + first, 0)),
                     pl.BlockSpec((D_MODEL, PROJ_WIDTH), const)]
        out_shape += [jax.ShapeDtypeStruct((rows_part, w), BF16) for _, w in _PROJ_COLS]
        out_specs += [pl.BlockSpec((ROW_TILE, w), row) for _, w in _PROJ_COLS]
    if out_part is not None:
        first_out = out_part * steps
        out_rows = lambda i: (i + first_out, 0)
        args += [x2d, *out_args]
        in_specs += [pl.BlockSpec((ROW_TILE, D_MODEL), out_rows),
                     pl.BlockSpec((ROW_TILE, GLA_WIDTH), row),
                     pl.BlockSpec((ROW_TILE, GLA_WIDTH), row),
                     pl.BlockSpec((ROW_TILE, FNET_WIDTH), row),
                     pl.BlockSpec((GLA_WIDTH + FNET_WIDTH, D_MODEL), const),
                     pl.BlockSpec((1, D_MODEL), const),
                     pl.BlockSpec((1, D_MODEL), const)]
        if out_prev is not None:
            aliases = {len(args): len(out_shape)}
            args.append(out_prev)
            in_specs.append(pl.BlockSpec(memory_space=pl.ANY))
        out_shape.append(jax.ShapeDtypeStruct((m, D_MODEL), F32))
        out_specs.append(pl.BlockSpec((ROW_TILE, D_MODEL), out_rows))
    body = functools.partial(_proj_stage_kernel, proj=proj_part is not None, out=out_part is not None,
                             carry_out=out_prev is not None)
    return pl.pallas_call(
        body,
        grid=(steps,),
        in_specs=in_specs,
        out_specs=out_specs,
        out_shape=out_shape,
        input_output_aliases=aliases,
        compiler_params=pltpu.CompilerParams(dimension_semantics=("arbitrary",),
                                             vmem_limit_bytes=VMEM_LIMIT_BYTES),
        name="proj_stage",
    )(*args)


def _permute_w_in(w):
    kw, gw, fw, r = GLA_KEY_WIDTH, GLA_WIDTH, FNET_WIDTH, GATE_RANK
    o_q, o_k, o_v = 0, kw, 2 * kw
    o_gf = o_v + gw
    o_zg = o_gf + 2 * r
    o_u = o_zg + gw
    o_zf = o_u + fw
    gates = jnp.concatenate([w[:, o_gf:o_gf + 2 * r]] * (GATE_PAD // (2 * r)), axis=1)
    cols = [w[:, o_q:o_q + kw] * (GLA_DK ** -0.5), w[:, o_k:o_k + kw], w[:, o_v:o_v + gw], w[:, o_zg:o_zg + gw],
            w[:, o_u:o_u + fw], w[:, o_zf:o_zf + fw], gates]
    return jnp.concatenate(cols, axis=1).astype(BF16)


def kernel(x, w_in, w_gate_up_fwd, b_gate_fwd, w_gate_up_bwd, b_gate_bwd, gla_norm_g,
           w_fnet, w_out, ln_g, ln_b):
    batch, seq, d = x.shape
    assert (seq, d) == (SEQ, D_MODEL) and w_in.shape[0] == DEPTH
    tables = _fnet_tables()
    for l in range(DEPTH):
        x2d = x.reshape(batch * seq, d)
        w_perm = _permute_w_in(w_in[l])
        w_out_b = w_out[l].astype(BF16)
        part_batch = batch // N_PARTS

        zero = jnp.zeros((GATE_RANK, GLA_KEY_WIDTH), F32)
        w_bd = jnp.block([[w_gate_up_fwd[l], zero], [zero, w_gate_up_bwd[l]]]) * LOG2_E
        w_hi = w_bd.astype(BF16)
        w_lo = (w_bd - w_hi.astype(F32)).astype(BF16)
        wg = jnp.concatenate([w_hi, w_hi, w_lo, jnp.zeros_like(w_hi)], axis=0)
        bg = jnp.concatenate([b_gate_fwd[l], b_gate_bwd[l]])[None, :] * LOG2_E
        seq_digits = (part_batch, RADIX, RADIX, RADIX, FNET_WIDTH)
        projected = _proj_stage(x2d, proj_part=0, w_perm=w_perm)
        out = None
        for part in range(N_PARTS):
            q, k, v, zg, u, zf, g = projected
            o_gla = _gla(q, k, v, g, wg, bg, gla_norm_g[l][None, :], part_batch)
            y_fnet = _fnet(u.reshape(seq_digits), zf.reshape(seq_digits), w_fnet[l], tables, part_batch)
            out_args = (o_gla, zg, y_fnet.reshape(part_batch * seq, FNET_WIDTH), w_out_b,
                        ln_g[l][None, :], ln_b[l][None, :])
            next_part = part + 1 if part + 1 < N_PARTS else None
            *projected, out = _proj_stage(x2d, proj_part=next_part, w_perm=w_perm, out_part=part,
                                          out_args=out_args, out_prev=out)
        x = out.reshape(batch, seq, d)
    return x
```

```python
import math

import numpy as np
import jax
import jax.numpy as jnp
from jax import lax
from jax.experimental import pallas as pl
from jax.experimental.pallas import tpu as pltpu

F32 = jnp.float32
BF16 = jnp.bfloat16

D_MODEL = 1024
SEQ = 4096
GLA_HEADS = 4
GLA_DK = 64
GLA_DV = 128
GLA_KEY_WIDTH = GLA_HEADS * GLA_DK
GLA_WIDTH = GLA_HEADS * GLA_DV
GATE_RANK = 16
GATE_TAU = 16.0
FNET_GROUPS = 4
FNET_GDIM = 128
FNET_WIDTH = FNET_GROUPS * FNET_GDIM
LN_EPS = 1e-5
RMS_EPS = 1e-6
DEPTH = 1
DEEPNORM_ALPHA = (2.0 * DEPTH) ** 0.25
LOG2_E = 1.0 / math.log(2.0)

LANES = 128
SUBLANES = 8
VMEM_LIMIT_BYTES = 60 * 1024 * 1024

ROW_TILE = 2048
OUT_ROW_TILE = 2048
OUT_SUBTILES = 8
GLA_CHUNK = 256
GATE_PAD = LANES
RADIX = 16
FNET_HALF = 256
GLA_SAFE_LOG2_DECAY = 100.0
GLA_PIPE_UNROLL = 15
GLA_UNROLL = 16
FNET_UNROLL = 32

_PROJ_COLS = (("q", GLA_KEY_WIDTH), ("k", GLA_KEY_WIDTH), ("v", GLA_WIDTH), ("zg", GLA_WIDTH),
              ("u", FNET_WIDTH), ("zf", FNET_WIDTH), ("g", GATE_PAD))
PROJ_WIDTH = sum(w for _, w in _PROJ_COLS)


def _dot(a, b):
    return jnp.dot(a, b, preferred_element_type=F32)


def _split2(x):
    hi = x.astype(BF16)
    lo = (x - hi.astype(F32)).astype(BF16)
    return hi, lo


def _dot_f32(a, b):
    ah, al = _split2(a)
    bh, bl = _split2(b)
    return _dot(ah, bh) + _dot(ah, bl) + _dot(al, bh)


def _silu(z):
    h = 0.5 * z
    return h + h * jnp.tanh(h)


def _proj_kernel(x_ref, w_ref, q_ref, k_ref, v_ref, zg_ref, u_ref, zf_ref, g_ref):
    xb = x_ref[...].astype(BF16)
    outs = (q_ref, k_ref, v_ref, zg_ref, u_ref, zf_ref, g_ref)
    off = 0
    for (name, width), o_ref in zip(_PROJ_COLS, outs):
        r = _dot(xb, w_ref[:, off:off + width])
        if name == "g":
            hi, lo = _split2(r)
            lane = lax.broadcasted_iota(jnp.int32, r.shape, 1)
            r = jnp.where((lane >= 2 * GATE_RANK) & (lane < 4 * GATE_RANK), lo, hi)
        o_ref[...] = r.astype(o_ref.dtype)
        off += width


def _projection(x2d, w_perm):
    m = x2d.shape[0]
    grid = (m // ROW_TILE,)
    row = lambda i: (i, 0)
    out_shape = [jax.ShapeDtypeStruct((m, w), BF16) for _, w in _PROJ_COLS]
    out_specs = [pl.BlockSpec((ROW_TILE, w), row) for _, w in _PROJ_COLS]
    return pl.pallas_call(
        _proj_kernel,
        grid=grid,
        in_specs=[pl.BlockSpec((ROW_TILE, D_MODEL), row),
                  pl.BlockSpec((D_MODEL, PROJ_WIDTH), lambda i: (0, 0))],
        out_specs=out_specs,
        out_shape=out_shape,
        compiler_params=pltpu.CompilerParams(dimension_semantics=("arbitrary",),
                                             vmem_limit_bytes=VMEM_LIMIT_BYTES),
        name="in_proj",
    )(x2d, w_perm)


def _gla_kernel(q_ref, k_ref, v_ref, g_ref, wg_ref, bg_ref, gam_ref, y_ref,
                oacc_ref, qd_ref, ki_ref, kdt_ref, upd_ref, dec_ref, dect_ref,
                flag_ref, bsc_ref, qsc_ref, psc_ref):
    C = GLA_CHUNK
    KW = GLA_KEY_WIDTH
    n_chunks = SEQ // C
    row_i = lax.broadcasted_iota(jnp.int32, (C, C), 0)
    col_i = lax.broadcasted_iota(jnp.int32, (C, C), 1)
    lower = row_i >= col_i
    upper = row_i <= col_i
    lane_head = lax.broadcasted_iota(jnp.int32, (C, KW), 1) // GLA_DK
    pair_top = lax.broadcasted_iota(jnp.int32, (2 * GLA_DK, GLA_DV), 0) < GLA_DK
    same_head = (lax.broadcasted_iota(jnp.int32, (2 * GLA_DV, 2 * GLA_DV), 0) // GLA_DV ==
                 lax.broadcasted_iota(jnp.int32, (2 * GLA_DV, 2 * GLA_DV), 1) // GLA_DV)
    pair_mean = jnp.where(same_head, 1.0 / GLA_DV, 0.0).astype(BF16)

    def chunk_rows(n):
        return pl.ds(pl.multiple_of(n * C, C), C)

    def gate_preact(rows):
        return _dot(g_ref[rows, :], wg_ref[...]) + bg_ref[...]

    def gate_logs(z2):
        tau_log2_a = jnp.minimum(z2, 0.0) - jnp.log2(1.0 + jnp.exp2(-jnp.abs(z2)))
        hi = tau_log2_a.astype(BF16)
        mid = (tau_log2_a - hi.astype(F32)).astype(BF16)
        return hi, mid

    def cum_log_decays(hi, mid):
        lo_t = jnp.where(lower, 1.0 / GATE_TAU, 0.0).astype(BF16)
        up_t = jnp.where(upper, 1.0 / GATE_TAU, 0.0).astype(BF16)
        b_f = _dot(jnp.concatenate([lo_t, lo_t], axis=1),
                   jnp.concatenate([hi[:, :KW], mid[:, :KW]], axis=0))
        b_b = _dot(jnp.concatenate([up_t, up_t], axis=1),
                   jnp.concatenate([hi[:, KW:], mid[:, KW:]], axis=0))
        return b_f, b_b

    def scale_qk(n, b_f, b_b):
        rows = chunk_rows(n)
        e_f = b_f[C - 1:C, :]
        e_b = b_b[0:1, :]
        dec_ref[n, 0] = jnp.broadcast_to(jnp.exp2(e_f), (SUBLANES, KW))
        dec_ref[n, 1] = jnp.broadcast_to(jnp.exp2(e_b), (SUBLANES, KW))
        flag_ref[n] = (jnp.max(-jnp.minimum(e_f, e_b)) > GLA_SAFE_LOG2_DECAY).astype(jnp.int32)

        qf = q_ref[rows, :].astype(F32)
        kf = k_ref[rows, :].astype(F32)
        qd_ref[rows, :KW] = (qf * jnp.exp2(b_f)).astype(BF16)
        qd_ref[rows, KW:] = (qf * jnp.exp2(b_b)).astype(BF16)
        ki_ref[rows, :KW] = (kf * jnp.exp2(-b_f)).astype(BF16)
        ki_ref[rows, KW:] = (kf * jnp.exp2(-b_b)).astype(BF16)
        kd = jnp.concatenate([kf * jnp.exp2(e_f - b_f), kf * jnp.exp2(e_b - b_b)], axis=1)
        kdt_ref[n] = jnp.transpose(kd).astype(BF16)

    def scores(n):
        rows = chunk_rows(n)
        nt = (((1,), (1,)), ((), ()))
        qd_f = qd_ref[rows, :KW]
        qd_b = qd_ref[rows, KW:]
        probs = []
        for h in range(GLA_HEADS):
            in_head = lane_head == h
            s_f = lax.dot_general(jnp.where(in_head, qd_f, jnp.zeros_like(qd_f)), ki_ref[rows, :KW],
                                  nt, preferred_element_type=F32)
            s_b = lax.dot_general(jnp.where(in_head, qd_b, jnp.zeros_like(qd_b)), ki_ref[rows, KW:],
                                  nt, preferred_element_type=F32)
            probs.append((jnp.where(lower, s_f, 0.0) + jnp.where(upper, s_b, 0.0)).astype(BF16))
        return probs

    def weigh_values(n, probs):
        rows = chunk_rows(n)
        for h in range(GLA_HEADS):
            vcols = slice(h * GLA_DV, (h + 1) * GLA_DV)
            lhs = jnp.concatenate([probs[h], kdt_ref[n, h * GLA_DK:(h + 1) * GLA_DK, :],
                                   kdt_ref[n, KW + h * GLA_DK:KW + (h + 1) * GLA_DK, :]], axis=0)
            r = _dot(lhs, v_ref[rows, vcols])
            oacc_ref[rows, vcols] = r[:C]
            upd_ref[n, 0, h * GLA_DK:(h + 1) * GLA_DK, :] = r[C:C + GLA_DK]
            upd_ref[n, 1, h * GLA_DK:(h + 1) * GLA_DK, :] = r[C + GLA_DK:]
        for d in range(2):
            dect_ref[n, d] = jnp.transpose(jnp.broadcast_to(dec_ref[n, d][0:1, :], (LANES, KW)))

    def attend_and_scale_next(n, carry):
        z2 = gate_preact(chunk_rows(n + 1))
        probs = scores(n)
        b_next = cum_log_decays(*gate_logs(z2))
        weigh_values(n, probs)
        scale_qk(n + 1, *b_next)
        return carry

    scale_qk(0, *cum_log_decays(*gate_logs(gate_preact(chunk_rows(0)))))
    lax.fori_loop(0, n_chunks - 1, attend_and_scale_next, 0, unroll=GLA_PIPE_UNROLL)
    weigh_values(n_chunks - 1, scores(n_chunks - 1))

    def redo_exact(n, carry):
        @pl.when(flag_ref[n] != 0)
        def _():
            rows = chunk_rows(n)
            nt = (((1,), (1,)), ((), ()))
            b_f, b_b = cum_log_decays(*gate_logs(gate_preact(rows)))
            bsc_ref[:, :KW] = b_f
            bsc_ref[:, KW:] = b_b
            qsc_ref[...] = q_ref[rows, :].astype(F32)
            col = lax.broadcasted_iota(jnp.int32, (SUBLANES, C), 1)
            head_row = (lax.broadcasted_iota(jnp.int32, (SUBLANES, KW), 0) ==
                        lax.broadcasted_iota(jnp.int32, (SUBLANES, KW), 1) // GLA_DK)

            def row_scores(i, c):
                kf = k_ref[rows, :].astype(F32)
                q_heads = jnp.where(head_row, jnp.broadcast_to(qsc_ref[pl.ds(i, 1), :], (SUBLANES, KW)),
                                    0.0).astype(BF16)
                w_f = jnp.exp2(jnp.minimum(bsc_ref[pl.ds(i, 1), :KW] - bsc_ref[:, :KW], 0.0))
                w_b = jnp.exp2(jnp.minimum(bsc_ref[pl.ds(i, 1), KW:] - bsc_ref[:, KW:], 0.0))
                s_f = lax.dot_general(q_heads, (kf * w_f).astype(BF16), nt, preferred_element_type=F32)
                s_b = lax.dot_general(q_heads, (kf * w_b).astype(BF16), nt, preferred_element_type=F32)
                p = jnp.where(col <= i, s_f, 0.0) + jnp.where(col >= i, s_b, 0.0)
                for h in range(GLA_HEADS):
                    psc_ref[h, pl.ds(i, 1), :] = p[h:h + 1, :]
                return c

            lax.fori_loop(0, C, row_scores, 0)
            for h in range(GLA_HEADS):
                vcols = slice(h * GLA_DV, (h + 1) * GLA_DV)
                oacc_ref[rows, vcols] = _dot(psc_ref[h].astype(BF16), v_ref[rows, vcols])
        return carry

    lax.fori_loop(0, n_chunks, redo_exact, 0)

    def enter_state(n, d, state):
        increment = upd_ref[n, d]
        upd_ref[n, d] = state
        return state * dect_ref[n, d] + increment

    def scan_states(j, states):
        s_f, s_b = states
        return enter_state(j, 0, s_f), enter_state(n_chunks - 1 - j, 1, s_b)

    state0 = jnp.zeros((KW, GLA_DV), F32)
    lax.fori_loop(0, n_chunks, scan_states, (state0, state0), unroll=GLA_UNROLL)

    def pair_operand(state_rows):
        sb = state_rows.astype(BF16)
        zero = jnp.zeros_like(sb)
        return jnp.concatenate([jnp.where(pair_top, sb, zero), jnp.where(pair_top, zero, sb)], axis=1)

    def cross_chunk_and_norm(n, carry):
        rows = chunk_rows(n)
        for pair in range(GLA_HEADS // 2):
            kdims = slice(2 * pair * GLA_DK, 2 * (pair + 1) * GLA_DK)
            cols = slice(2 * pair * GLA_DV, 2 * (pair + 1) * GLA_DV)
            q_cat = jnp.concatenate([qd_ref[rows, kdims],
                                     qd_ref[rows, KW + 2 * pair * GLA_DK:KW + 2 * (pair + 1) * GLA_DK]],
                                    axis=1)
            s_cat = jnp.concatenate([pair_operand(upd_ref[n, 0, kdims, :]),
                                     pair_operand(upd_ref[n, 1, kdims, :])], axis=0)
            o_p = oacc_ref[rows, cols] + _dot(q_cat, s_cat)
            ms = _dot((o_p * o_p).astype(BF16), pair_mean)
            y_ref[rows, cols] = (o_p * lax.rsqrt(ms + RMS_EPS) * gam_ref[:, cols]).astype(y_ref.dtype)
        return carry

    lax.fori_loop(0, n_chunks, cross_chunk_and_norm, 0, unroll=GLA_UNROLL)


def _gla(q, k, v, g, wg, bg, gamma, batch):
    tok = lambda b: (b, 0)
    const = lambda b: (0, 0)
    n_chunks = SEQ // GLA_CHUNK
    return pl.pallas_call(
        _gla_kernel,
        grid=(batch,),
        in_specs=[pl.BlockSpec((SEQ, GLA_KEY_WIDTH), tok),
                  pl.BlockSpec((SEQ, GLA_KEY_WIDTH), tok),
                  pl.BlockSpec((SEQ, GLA_WIDTH), tok),
                  pl.BlockSpec((SEQ, GATE_PAD), tok),
                  pl.BlockSpec((GATE_PAD, 2 * GLA_KEY_WIDTH), const),
                  pl.BlockSpec((1, 2 * GLA_KEY_WIDTH), const),
                  pl.BlockSpec((1, GLA_WIDTH), const)],
        out_specs=pl.BlockSpec((SEQ, GLA_WIDTH), tok),
        out_shape=jax.ShapeDtypeStruct((batch * SEQ, GLA_WIDTH), BF16),
        scratch_shapes=[pltpu.VMEM((SEQ, GLA_WIDTH), F32),
                        pltpu.VMEM((SEQ, 2 * GLA_KEY_WIDTH), BF16),
                        pltpu.VMEM((SEQ, 2 * GLA_KEY_WIDTH), BF16),
                        pltpu.VMEM((n_chunks, 2 * GLA_KEY_WIDTH, GLA_CHUNK), BF16),
                        pltpu.VMEM((n_chunks, 2, GLA_KEY_WIDTH, GLA_DV), F32),
                        pltpu.VMEM((n_chunks, 2, SUBLANES, GLA_KEY_WIDTH), F32),
                        pltpu.VMEM((n_chunks, 2, GLA_KEY_WIDTH, LANES), F32),
                        pltpu.SMEM((n_chunks,), jnp.int32),
                        pltpu.VMEM((GLA_CHUNK, 2 * GLA_KEY_WIDTH), F32),
                        pltpu.VMEM((GLA_CHUNK, GLA_KEY_WIDTH), F32),
                        pltpu.VMEM((GLA_HEADS, GLA_CHUNK, GLA_CHUNK), F32)],
        compiler_params=pltpu.CompilerParams(dimension_semantics=("arbitrary",),
                                             vmem_limit_bytes=VMEM_LIMIT_BYTES),
        name="gla",
    )(q, k, v, g, wg, bg, gamma)


def _fnet_tables():
    R = RADIX
    n = SEQ
    a = np.arange(R)
    lo8 = np.arange(SUBLANES)
    eye8 = np.eye(SUBLANES)

    ka = (8 * np.arange(2)[:, None] + lo8[None, :])
    ang = 2 * np.pi * (256 * a[None, None, :] * ka[:, :, None]) / n
    eye16 = np.eye(R)
    def s1(trig):
        t = trig(ang)
        m = t[:, None, :, :, None] * eye16[None, :, None, None, :]
        return m.reshape(2 * R * SUBLANES, R * R)
    m1 = np.concatenate([s1(np.cos), -s1(np.sin)], axis=0)

    def complex_block(theta):
        c = np.cos(theta)
        s = -np.sin(theta)
        def kron(t):
            m = t[:, :, :, None] * eye8[None, :, None, :]
            return m.reshape(t.shape[0] * SUBLANES, t.shape[2] * SUBLANES)
        mr, mi = kron(c), kron(s)
        return np.block([[mr, -mi], [mi, mr]])

    m2 = []
    for kah in range(2):
        k_a = 8 * kah + lo8
        expo = 16 * a[None, None, :] * k_a[None, :, None] + 256 * a[None, None, :] * a[:, None, None]
        m2.append(complex_block(2 * np.pi * expo / n))
    m2 = np.stack(m2)

    m3 = np.zeros((2, R, 2 * R * SUBLANES, 2 * R * SUBLANES))
    for kah in range(2):
        k_a = 8 * kah + lo8
        for kb in range(R):
            expo = (a[None, None, :] * k_a[None, :, None] + 16 * a[None, None, :] * kb
                    + 256 * a[None, None, :] * a[:, None, None])
            m3[kah, kb] = complex_block(2 * np.pi * expo / n)

    c = np.arange(FNET_GDIM)
    ang_c = 2 * np.pi * np.outer(c, c) / FNET_GDIM
    norm = 1.0 / math.sqrt(SEQ * FNET_GDIM)
    chan = np.concatenate([np.cos(ang_c), np.sin(ang_c)], axis=0) * norm
    return jnp.asarray(m1, F32), jnp.asarray(m2, F32), jnp.asarray(m3, F32), jnp.asarray(chan, F32)


def _fnet_kernel(u_ref, zf_ref, wf_ref, m1_ref, m2_ref, m3_ref, chan_ref, out_ref,
                 y1r, y1i, y2r, y2i, mg_ref):
    R = RADIX
    T = SUBLANES
    W = FNET_HALF

    @pl.when(pl.program_id(0) == 0)
    def _():
        for g in range(FNET_GROUPS):
            mg_ref[g] = _dot_f32(chan_ref[...], wf_ref[g]).astype(BF16)

    for half in range(FNET_WIDTH // W):
        lanes = slice(half * W, (half + 1) * W)

        def stage1(sm, carry):
            blk = u_ref[:, sm, :, lanes].reshape(R * R, W)
            res = _dot(m1_ref[...].astype(BF16), blk)
            y1r[:, :, sm, :, :] = res[:R * R].reshape(2, R, T, W)
            y1i[:, :, sm, :, :] = res[R * R:].reshape(2, R, T, W)
            return carry

        lax.fori_loop(0, R, stage1, 0, unroll=FNET_UNROLL)

        def stage2(i, carry):
            kah = i // R
            sl = i % R
            rhs = jnp.concatenate([y1r[kah, sl].reshape(R * T, W), y1i[kah, sl].reshape(R * T, W)],
                                  axis=0).astype(BF16)
            res = _dot(m2_ref[kah].astype(BF16), rhs)
            y2r[kah, :, sl, :, :] = res[:R * T].reshape(R, T, W)
            y2i[kah, :, sl, :, :] = res[R * T:].reshape(R, T, W)
            return carry

        lax.fori_loop(0, 2 * R, stage2, 0, unroll=FNET_UNROLL)

        def stage3(kb, carry):
            mixed = []
            for kah in range(2):
                rhs = jnp.concatenate([y2r[kah, kb].reshape(R * T, W), y2i[kah, kb].reshape(R * T, W)],
                                      axis=0).astype(BF16)
                res = _dot(m3_ref[kah, kb].astype(BF16), rhs)
                groups = []
                for j in range(W // FNET_GDIM):
                    gl = slice(j * FNET_GDIM, (j + 1) * FNET_GDIM)
                    z = jnp.concatenate([res[:R * T, gl], res[R * T:, gl]], axis=1).astype(BF16)
                    groups.append(_dot(z, mg_ref[half * (W // FNET_GDIM) + j]))
                mixed.append(jnp.concatenate(groups, axis=1).reshape(R, T, W))
            uf = jnp.stack(mixed, axis=1).reshape(R, 2 * T, W)
            gate = _silu(zf_ref[:, kb, :, lanes].astype(F32))
            out_ref[:, kb, :, lanes] = (uf * gate).astype(out_ref.dtype)
            return carry

        lax.fori_loop(0, R, stage3, 0, unroll=FNET_UNROLL)


def _fnet(u5, zf5, w_fnet, tables, batch):
    m1, m2, m3, chan = tables
    R, T, W = RADIX, SUBLANES, FNET_HALF
    return pl.pallas_call(
        _fnet_kernel,
        grid=(batch,),
        in_specs=[pl.BlockSpec((None, R, R, R, FNET_WIDTH), lambda b: (b, 0, 0, 0, 0)),
                  pl.BlockSpec((None, R, R, R, FNET_WIDTH), lambda b: (b, 0, 0, 0, 0)),
                  pl.BlockSpec((FNET_GROUPS, FNET_GDIM, FNET_GDIM), lambda b: (0, 0, 0)),
                  pl.BlockSpec(m1.shape, lambda b: (0, 0)),
                  pl.BlockSpec(m2.shape, lambda b: (0, 0, 0)),
                  pl.BlockSpec(m3.shape, lambda b: (0, 0, 0, 0)),
                  pl.BlockSpec(chan.shape, lambda b: (0, 0))],
        out_specs=pl.BlockSpec((None, R, R, R, FNET_WIDTH), lambda b: (b, 0, 0, 0, 0)),
        out_shape=jax.ShapeDtypeStruct((batch, R, R, R, FNET_WIDTH), BF16),
        scratch_shapes=[pltpu.VMEM((2, R, R, T, W), F32), pltpu.VMEM((2, R, R, T, W), F32),
                        pltpu.VMEM((2, R, R, T, W), F32), pltpu.VMEM((2, R, R, T, W), F32),
                        pltpu.VMEM((FNET_GROUPS, 2 * FNET_GDIM, FNET_GDIM), BF16)],
        compiler_params=pltpu.CompilerParams(dimension_semantics=("arbitrary",),
                                             vmem_limit_bytes=VMEM_LIMIT_BYTES),
        name="fnet",
    )(u5, zf5, w_fnet, m1, m2, m3, chan)


def _out_kernel(x_ref, og_ref, zg_ref, yf_ref, w_ref, lng_ref, lnb_ref, o_ref):
    sub = OUT_ROW_TILE // OUT_SUBTILES
    ys = []
    for t in range(OUT_SUBTILES):
        rows = slice(t * sub, (t + 1) * sub)
        y_gla = (og_ref[rows, :].astype(F32) * _silu(zg_ref[rows, :].astype(F32))).astype(BF16)
        ys.append(_dot(jnp.concatenate([y_gla, yf_ref[rows, :]], axis=1), w_ref[...]))
    for t in range(OUT_SUBTILES):
        rows = slice(t * sub, (t + 1) * sub)
        r = DEEPNORM_ALPHA * x_ref[rows, :] + ys[t]
        mu = jnp.mean(r, axis=-1, keepdims=True)
        d = r - mu
        var = jnp.mean(d * d, axis=-1, keepdims=True)
        o_ref[rows, :] = d * lax.rsqrt(var + LN_EPS) * lng_ref[...] + lnb_ref[...]


def _out_proj(x2d, o_gla, zg, y_fnet, w_out, ln_g, ln_b):
    m = x2d.shape[0]
    row = lambda i: (i, 0)
    const = lambda i: (0, 0)
    return pl.pallas_call(
        _out_kernel,
        grid=(m // OUT_ROW_TILE,),
        in_specs=[pl.BlockSpec((OUT_ROW_TILE, D_MODEL), row),
                  pl.BlockSpec((OUT_ROW_TILE, GLA_WIDTH), row),
                  pl.BlockSpec((OUT_ROW_TILE, GLA_WIDTH), row),
                  pl.BlockSpec((OUT_ROW_TILE, FNET_WIDTH), row),
                  pl.BlockSpec((GLA_WIDTH + FNET_WIDTH, D_MODEL), const),
                  pl.BlockSpec((1, D_MODEL), const),
                  pl.BlockSpec((1, D_MODEL), const)],
        out_specs=pl.BlockSpec((OUT_ROW_TILE, D_MODEL), row),
        out_shape=jax.ShapeDtypeStruct((m, D_MODEL), F32),
        compiler_params=pltpu.CompilerParams(dimension_semantics=("arbitrary",),
                                             vmem_limit_bytes=VMEM_LIMIT_BYTES),
        name="out_proj",
    )(x2d, o_gla, zg, y_fnet, w_out, ln_g, ln_b)


def _permute_w_in(w):
    kw, gw, fw, r = GLA_KEY_WIDTH, GLA_WIDTH, FNET_WIDTH, GATE_RANK
    o_q, o_k, o_v = 0, kw, 2 * kw
    o_gf = o_v + gw
    o_zg = o_gf + 2 * r
    o_u = o_zg + gw
    o_zf = o_u + fw
    gates = jnp.concatenate([w[:, o_gf:o_gf + 2 * r]] * (GATE_PAD // (2 * r)), axis=1)
    cols = [w[:, o_q:o_q + kw] * (GLA_DK ** -0.5), w[:, o_k:o_k + kw], w[:, o_v:o_v + gw], w[:, o_zg:o_zg + gw],
            w[:, o_u:o_u + fw], w[:, o_zf:o_zf + fw], gates]
    return jnp.concatenate(cols, axis=1).astype(BF16)


def kernel(x, w_in, w_gate_up_fwd, b_gate_fwd, w_gate_up_bwd, b_gate_bwd, gla_norm_g,
           w_fnet, w_out, ln_g, ln_b):
    batch, seq, d = x.shape
    assert (seq, d) == (SEQ, D_MODEL) and w_in.shape[0] == DEPTH
    tables = _fnet_tables()
    for l in range(DEPTH):
        x2d = x.reshape(batch * seq, d)
        q, k, v, zg, u, zf, g = _projection(x2d, _permute_w_in(w_in[l]))

        zero = jnp.zeros((GATE_RANK, GLA_KEY_WIDTH), F32)
        w_bd = jnp.block([[w_gate_up_fwd[l], zero], [zero, w_gate_up_bwd[l]]]) * LOG2_E
        w_hi = w_bd.astype(BF16)
        w_lo = (w_bd - w_hi.astype(F32)).astype(BF16)
        wg = jnp.concatenate([w_hi, w_hi, w_lo, jnp.zeros_like(w_hi)], axis=0)
        bg = jnp.concatenate([b_gate_fwd[l], b_gate_bwd[l]])[None, :] * LOG2_E
        o_gla = _gla(q, k, v, g, wg, bg, gla_norm_g[l][None, :], batch)

        seq_digits = (batch, RADIX, RADIX, RADIX, FNET_WIDTH)
        y_fnet = _fnet(u.reshape(seq_digits), zf.reshape(seq_digits), w_fnet[l], tables, batch)

        out = _out_proj(x2d, o_gla, zg, y_fnet.reshape(batch * seq, FNET_WIDTH),
                        w_out[l].astype(BF16), ln_g[l][None, :], ln_b[l][None, :])
        x = out.reshape(batch, seq, d)
    return x
```

```python
import math

import numpy as np
import jax
import jax.numpy as jnp
from jax import lax
from jax.experimental import pallas as pl
from jax.experimental.pallas import tpu as pltpu

F32 = jnp.float32
BF16 = jnp.bfloat16

D_MODEL = 1024
SEQ = 4096
GLA_HEADS = 4
GLA_DK = 64
GLA_DV = 128
GLA_KEY_WIDTH = GLA_HEADS * GLA_DK
GLA_WIDTH = GLA_HEADS * GLA_DV
GATE_RANK = 16
GATE_TAU = 16.0
FNET_GROUPS = 4
FNET_GDIM = 128
FNET_WIDTH = FNET_GROUPS * FNET_GDIM
LN_EPS = 1e-5
RMS_EPS = 1e-6
DEPTH = 1
DEEPNORM_ALPHA = (2.0 * DEPTH) ** 0.25
LOG2_E = 1.0 / math.log(2.0)

LANES = 128
SUBLANES = 8
VMEM_LIMIT_BYTES = 60 * 1024 * 1024

ROW_TILE = 2048
OUT_ROW_TILE = 2048
OUT_SUBTILES = 8
GLA_CHUNK = 256
GATE_PAD = LANES
RADIX = 16
FNET_HALF = 256
GLA_SAFE_LOG2_DECAY = 100.0
GLA_PIPE_UNROLL = 15
GLA_UNROLL = 16
FNET_UNROLL = 32

_PROJ_COLS = (("q", GLA_KEY_WIDTH), ("k", GLA_KEY_WIDTH), ("v", GLA_WIDTH), ("zg", GLA_WIDTH),
              ("u", FNET_WIDTH), ("zf", FNET_WIDTH), ("g", GATE_PAD))
PROJ_WIDTH = sum(w for _, w in _PROJ_COLS)


def _dot(a, b):
    return jnp.dot(a, b, preferred_element_type=F32)


def _split2(x):
    hi = x.astype(BF16)
    lo = (x - hi.astype(F32)).astype(BF16)
    return hi, lo


def _dot_f32(a, b):
    ah, al = _split2(a)
    bh, bl = _split2(b)
    return _dot(ah, bh) + _dot(ah, bl) + _dot(al, bh)


def _silu(z):
    h = 0.5 * z
    return h + h * jnp.tanh(h)


def _proj_kernel(x_ref, w_ref, q_ref, k_ref, v_ref, zg_ref, u_ref, zf_ref, g_ref):
    xb = x_ref[...].astype(BF16)
    outs = (q_ref, k_ref, v_ref, zg_ref, u_ref, zf_ref, g_ref)
    off = 0
    for (name, width), o_ref in zip(_PROJ_COLS, outs):
        r = _dot(xb, w_ref[:, off:off + width])
        if name == "g":
            hi, lo = _split2(r)
            lane = lax.broadcasted_iota(jnp.int32, r.shape, 1)
            r = jnp.where((lane >= 2 * GATE_RANK) & (lane < 4 * GATE_RANK), lo, hi)
        o_ref[...] = r.astype(o_ref.dtype)
        off += width


def _projection(x2d, w_perm):
    m = x2d.shape[0]
    grid = (m // ROW_TILE,)
    row = lambda i: (i, 0)
    out_shape = [jax.ShapeDtypeStruct((m, w), BF16) for _, w in _PROJ_COLS]
    out_specs = [pl.BlockSpec((ROW_TILE, w), row) for _, w in _PROJ_COLS]
    return pl.pallas_call(
        _proj_kernel,
        grid=grid,
        in_specs=[pl.BlockSpec((ROW_TILE, D_MODEL), row),
                  pl.BlockSpec((D_MODEL, PROJ_WIDTH), lambda i: (0, 0))],
        out_specs=out_specs,
        out_shape=out_shape,
        compiler_params=pltpu.CompilerParams(dimension_semantics=("arbitrary",),
                                             vmem_limit_bytes=VMEM_LIMIT_BYTES),
        name="in_proj",
    )(x2d, w_perm)


def _gla_kernel(q_ref, k_ref, v_ref, g_ref, wg_ref, bg_ref, gam_ref, y_ref,
                oacc_ref, qd_ref, ki_ref, kdt_ref, upd_ref, dec_ref, dect_ref,
                flag_ref, bsc_ref, qsc_ref, psc_ref):
    C = GLA_CHUNK
    KW = GLA_KEY_WIDTH
    n_chunks = SEQ // C
    row_i = lax.broadcasted_iota(jnp.int32, (C, C), 0)
    col_i = lax.broadcasted_iota(jnp.int32, (C, C), 1)
    lower = row_i >= col_i
    upper = row_i <= col_i
    lane_head = lax.broadcasted_iota(jnp.int32, (C, KW), 1) // GLA_DK
    pair_top = lax.broadcasted_iota(jnp.int32, (2 * GLA_DK, GLA_DV), 0) < GLA_DK
    same_head = (lax.broadcasted_iota(jnp.int32, (2 * GLA_DV, 2 * GLA_DV), 0) // GLA_DV ==
                 lax.broadcasted_iota(jnp.int32, (2 * GLA_DV, 2 * GLA_DV), 1) // GLA_DV)
    pair_mean = jnp.where(same_head, 1.0 / GLA_DV, 0.0).astype(BF16)

    def chunk_rows(n):
        return pl.ds(pl.multiple_of(n * C, C), C)

    def gate_preact(rows):
        return _dot(g_ref[rows, :], wg_ref[...]) + bg_ref[...]

    def gate_logs(z2):
        tau_log2_a = jnp.minimum(z2, 0.0) - jnp.log2(1.0 + jnp.exp2(-jnp.abs(z2)))
        hi = tau_log2_a.astype(BF16)
        mid = (tau_log2_a - hi.astype(F32)).astype(BF16)
        return hi, mid

    def cum_log_decays(hi, mid):
        lo_t = jnp.where(lower, 1.0 / GATE_TAU, 0.0).astype(BF16)
        up_t = jnp.where(upper, 1.0 / GATE_TAU, 0.0).astype(BF16)
        b_f = _dot(jnp.concatenate([lo_t, lo_t], axis=1),
                   jnp.concatenate([hi[:, :KW], mid[:, :KW]], axis=0))
        b_b = _dot(jnp.concatenate([up_t, up_t], axis=1),
                   jnp.concatenate([hi[:, KW:], mid[:, KW:]], axis=0))
        return b_f, b_b

    def scale_qk(n, b_f, b_b):
        rows = chunk_rows(n)
        e_f = b_f[C - 1:C, :]
        e_b = b_b[0:1, :]
        dec_ref[n, 0] = jnp.broadcast_to(jnp.exp2(e_f), (SUBLANES, KW))
        dec_ref[n, 1] = jnp.broadcast_to(jnp.exp2(e_b), (SUBLANES, KW))
        flag_ref[n] = (jnp.max(-jnp.minimum(e_f, e_b)) > GLA_SAFE_LOG2_DECAY).astype(jnp.int32)

        qf = q_ref[rows, :].astype(F32)
        kf = k_ref[rows, :].astype(F32)
        qd_ref[rows, :KW] = (qf * jnp.exp2(b_f)).astype(BF16)
        qd_ref[rows, KW:] = (qf * jnp.exp2(b_b)).astype(BF16)
        ki_ref[rows, :KW] = (kf * jnp.exp2(-b_f)).astype(BF16)
        ki_ref[rows, KW:] = (kf * jnp.exp2(-b_b)).astype(BF16)
        kd = jnp.concatenate([kf * jnp.exp2(e_f - b_f), kf * jnp.exp2(e_b - b_b)], axis=1)
        kdt_ref[n] = jnp.transpose(kd).astype(BF16)

    def scores(n):
        rows = chunk_rows(n)
        nt = (((1,), (1,)), ((), ()))
        qd_f = qd_ref[rows, :KW]
        qd_b = qd_ref[rows, KW:]
        probs = []
        for h in range(GLA_HEADS):
            in_head = lane_head == h
            s_f = lax.dot_general(jnp.where(in_head, qd_f, jnp.zeros_like(qd_f)), ki_ref[rows, :KW],
                                  nt, preferred_element_type=F32)
            s_b = lax.dot_general(jnp.where(in_head, qd_b, jnp.zeros_like(qd_b)), ki_ref[rows, KW:],
                                  nt, preferred_element_type=F32)
            probs.append((jnp.where(lower, s_f, 0.0) + jnp.where(upper, s_b, 0.0)).astype(BF16))
        return probs

    def weigh_values(n, probs):
        rows = chunk_rows(n)
        for h in range(GLA_HEADS):
            vcols = slice(h * GLA_DV, (h + 1) * GLA_DV)
            lhs = jnp.concatenate([probs[h], kdt_ref[n, h * GLA_DK:(h + 1) * GLA_DK, :],
                                   kdt_ref[n, KW + h * GLA_DK:KW + (h + 1) * GLA_DK, :]], axis=0)
            r = _dot(lhs, v_ref[rows, vcols])
            oacc_ref[rows, vcols] = r[:C]
            upd_ref[n, 0, h * GLA_DK:(h + 1) * GLA_DK, :] = r[C:C + GLA_DK]
            upd_ref[n, 1, h * GLA_DK:(h + 1) * GLA_DK, :] = r[C + GLA_DK:]
        for d in range(2):
            dect_ref[n, d] = jnp.transpose(jnp.broadcast_to(dec_ref[n, d][0:1, :], (LANES, KW)))

    def attend_and_scale_next(n, carry):
        z2 = gate_preact(chunk_rows(n + 1))
        probs = scores(n)
        b_next = cum_log_decays(*gate_logs(z2))
        weigh_values(n, probs)
        scale_qk(n + 1, *b_next)
        return carry

    scale_qk(0, *cum_log_decays(*gate_logs(gate_preact(chunk_rows(0)))))
    lax.fori_loop(0, n_chunks - 1, attend_and_scale_next, 0, unroll=GLA_PIPE_UNROLL)
    weigh_values(n_chunks - 1, scores(n_chunks - 1))

    def redo_exact(n, carry):
        @pl.when(flag_ref[n] != 0)
        def _():
            rows = chunk_rows(n)
            nt = (((1,), (1,)), ((), ()))
            b_f, b_b = cum_log_decays(*gate_logs(gate_preact(rows)))
            bsc_ref[:, :KW] = b_f
            bsc_ref[:, KW:] = b_b
            qsc_ref[...] = q_ref[rows, :].astype(F32)
            col = lax.broadcasted_iota(jnp.int32, (SUBLANES, C), 1)
            head_row = (lax.broadcasted_iota(jnp.int32, (SUBLANES, KW), 0) ==
                        lax.broadcasted_iota(jnp.int32, (SUBLANES, KW), 1) // GLA_DK)

            def row_scores(i, c):
                kf = k_ref[rows, :].astype(F32)
                q_heads = jnp.where(head_row, jnp.broadcast_to(qsc_ref[pl.ds(i, 1), :], (SUBLANES, KW)),
                                    0.0).astype(BF16)
                w_f = jnp.exp2(jnp.minimum(bsc_ref[pl.ds(i, 1), :KW] - bsc_ref[:, :KW], 0.0))
                w_b = jnp.exp2(jnp.minimum(bsc_ref[pl.ds(i, 1), KW:] - bsc_ref[:, KW:], 0.0))
                s_f = lax.dot_general(q_heads, (kf * w_f).astype(BF16), nt, preferred_element_type=F32)
                s_b = lax.dot_general(q_heads, (kf * w_b).astype(BF16), nt, preferred_element_type=F32)
                p = jnp.where(col <= i, s_f, 0.0) + jnp.where(col >= i, s_b, 0.0)
                for h in range(GLA_HEADS):
                    psc_ref[h, pl.ds(i, 1), :] = p[h:h + 1, :]
                return c

            lax.fori_loop(0, C, row_scores, 0)
            for h in range(GLA_HEADS):
                vcols = slice(h * GLA_DV, (h + 1) * GLA_DV)
                oacc_ref[rows, vcols] = _dot(psc_ref[h].astype(BF16), v_ref[rows, vcols])
        return carry

    lax.fori_loop(0, n_chunks, redo_exact, 0)

    def enter_state(n, d, state):
        increment = upd_ref[n, d]
        upd_ref[n, d] = state
        return state * dect_ref[n, d] + increment

    def scan_states(j, states):
        s_f, s_b = states
        return enter_state(j, 0, s_f), enter_state(n_chunks - 1 - j, 1, s_b)

    state0 = jnp.zeros((KW, GLA_DV), F32)
    lax.fori_loop(0, n_chunks, scan_states, (state0, state0), unroll=GLA_UNROLL)

    def pair_operand(state_rows):
        sb = state_rows.astype(BF16)
        zero = jnp.zeros_like(sb)
        return jnp.concatenate([jnp.where(pair_top, sb, zero), jnp.where(pair_top, zero, sb)], axis=1)

    def cross_chunk_and_norm(n, carry):
        rows = chunk_rows(n)
        for pair in range(GLA_HEADS // 2):
            kdims = slice(2 * pair * GLA_DK, 2 * (pair + 1) * GLA_DK)
            cols = slice(2 * pair * GLA_DV, 2 * (pair + 1) * GLA_DV)
            q_cat = jnp.concatenate([qd_ref[rows, kdims],
                                     qd_ref[rows, KW + 2 * pair * GLA_DK:KW + 2 * (pair + 1) * GLA_DK]],
                                    axis=1)
            s_cat = jnp.concatenate([pair_operand(upd_ref[n, 0, kdims, :]),
                                     pair_operand(upd_ref[n, 1, kdims, :])], axis=0)
            o_p = oacc_ref[rows, cols] + _dot(q_cat, s_cat)
            ms = _dot((o_p * o_p).astype(BF16), pair_mean)
            y_ref[rows, cols] = (o_p * lax.rsqrt(ms + RMS_EPS) * gam_ref[:, cols]).astype(y_ref.dtype)
        return carry

    lax.fori_loop(0, n_chunks, cross_chunk_and_norm, 0, unroll=GLA_UNROLL)


def _gla(q, k, v, g, wg, bg, gamma, batch):
    tok = lambda b: (b, 0)
    const = lambda b: (0, 0)
    n_chunks = SEQ // GLA_CHUNK
    return pl.pallas_call(
        _gla_kernel,
        grid=(batch,),
        in_specs=[pl.BlockSpec((SEQ, GLA_KEY_WIDTH), tok),
                  pl.BlockSpec((SEQ, GLA_KEY_WIDTH), tok),
                  pl.BlockSpec((SEQ, GLA_WIDTH), tok),
                  pl.BlockSpec((SEQ, GATE_PAD), tok),
                  pl.BlockSpec((GATE_PAD, 2 * GLA_KEY_WIDTH), const),
                  pl.BlockSpec((1, 2 * GLA_KEY_WIDTH), const),
                  pl.BlockSpec((1, GLA_WIDTH), const)],
        out_specs=pl.BlockSpec((SEQ, GLA_WIDTH), tok),
        out_shape=jax.ShapeDtypeStruct((batch * SEQ, GLA_WIDTH), BF16),
        scratch_shapes=[pltpu.VMEM((SEQ, GLA_WIDTH), F32),
                        pltpu.VMEM((SEQ, 2 * GLA_KEY_WIDTH), BF16),
                        pltpu.VMEM((SEQ, 2 * GLA_KEY_WIDTH), BF16),
                        pltpu.VMEM((n_chunks, 2 * GLA_KEY_WIDTH, GLA_CHUNK), BF16),
                        pltpu.VMEM((n_chunks, 2, GLA_KEY_WIDTH, GLA_DV), F32),
                        pltpu.VMEM((n_chunks, 2, SUBLANES, GLA_KEY_WIDTH), F32),
                        pltpu.VMEM((n_chunks, 2, GLA_KEY_WIDTH, LANES), F32),
                        pltpu.SMEM((n_chunks,), jnp.int32),
                        pltpu.VMEM((GLA_CHUNK, 2 * GLA_KEY_WIDTH), F32),
                        pltpu.VMEM((GLA_CHUNK, GLA_KEY_WIDTH), F32),
                        pltpu.VMEM((GLA_HEADS, GLA_CHUNK, GLA_CHUNK), F32)],
        compiler_params=pltpu.CompilerParams(dimension_semantics=("arbitrary",),
                                             vmem_limit_bytes=VMEM_LIMIT_BYTES),
        name="gla",
    )(q, k, v, g, wg, bg, gamma)


def _fnet_tables():
    R = RADIX
    n = SEQ
    a = np.arange(R)
    lo8 = np.arange(SUBLANES)
    eye8 = np.eye(SUBLANES)

    ka = (8 * np.arange(2)[:, None] + lo8[None, :])
    ang = 2 * np.pi * (256 * a[None, None, :] * ka[:, :, None]) / n
    eye16 = np.eye(R)
    def s1(trig):
        t = trig(ang)
        m = t[:, None, :, :, None] * eye16[None, :, None, None, :]
        return m.reshape(2 * R * SUBLANES, R * R)
    m1 = np.concatenate([s1(np.cos), -s1(np.sin)], axis=0)

    def complex_block(theta):
        c = np.cos(theta)
        s = -np.sin(theta)
        def kron(t):
            m = t[:, :, :, None] * eye8[None, :, None, :]
            return m.reshape(t.shape[0] * SUBLANES, t.shape[2] * SUBLANES)
        mr, mi = kron(c), kron(s)
        return np.block([[mr, -mi], [mi, mr]])

    m2 = []
    for kah in range(2):
        k_a = 8 * kah + lo8
        expo = 16 * a[None, None, :] * k_a[None, :, None] + 256 * a[None, None, :] * a[:, None, None]
        m2.append(complex_block(2 * np.pi * expo / n))
    m2 = np.stack(m2)

    m3 = np.zeros((2, R, 2 * R * SUBLANES, 2 * R * SUBLANES))
    for kah in range(2):
        k_a = 8 * kah + lo8
        for kb in range(R):
            expo = (a[None, None, :] * k_a[None, :, None] + 16 * a[None, None, :] * kb
                    + 256 * a[None, None, :] * a[:, None, None])
            m3[kah, kb] = complex_block(2 * np.pi * expo / n)

    c = np.arange(FNET_GDIM)
    ang_c = 2 * np.pi * np.outer(c, c) / FNET_GDIM
    norm = 1.0 / math.sqrt(SEQ * FNET_GDIM)
    chan = np.concatenate([np.cos(ang_c), np.sin(ang_c)], axis=0) * norm
    return jnp.asarray(m1, F32), jnp.asarray(m2, F32), jnp.asarray(m3, F32), jnp.asarray(chan, F32)


def _fnet_kernel(u_ref, zf_ref, wf_ref, m1_ref, m2_ref, m3_ref, chan_ref, out_ref,
                 y1r, y1i, y2r, y2i, mg_ref):
    R = RADIX
    T = SUBLANES
    W = FNET_HALF

    @pl.when(pl.program_id(0) == 0)
    def _():
        for g in range(FNET_GROUPS):
            mg_ref[g] = _dot_f32(chan_ref[...], wf_ref[g]).astype(BF16)

    for half in range(FNET_WIDTH // W):
        lanes = slice(half * W, (half + 1) * W)

        def stage1(sm, carry):
            blk = u_ref[:, sm, :, lanes].reshape(R * R, W)
            res = _dot(m1_ref[...].astype(BF16), blk)
            y1r[:, :, sm, :, :] = res[:R * R].reshape(2, R, T, W)
            y1i[:, :, sm, :, :] = res[R * R:].reshape(2, R, T, W)
            return carry

        lax.fori_loop(0, R, stage1, 0, unroll=FNET_UNROLL)

        def stage2(i, carry):
            kah = i // R
            sl = i % R
            rhs = jnp.concatenate([y1r[kah, sl].reshape(R * T, W), y1i[kah, sl].reshape(R * T, W)],
                                  axis=0).astype(BF16)
            res = _dot(m2_ref[kah].astype(BF16), rhs)
            y2r[kah, :, sl, :, :] = res[:R * T].reshape(R, T, W)
            y2i[kah, :, sl, :, :] = res[R * T:].reshape(R, T, W)
            return carry

        lax.fori_loop(0, 2 * R, stage2, 0, unroll=FNET_UNROLL)

        def stage3(kb, carry):
            mixed = []
            for kah in range(2):
                rhs = jnp.concatenate([y2r[kah, kb].reshape(R * T, W), y2i[kah, kb].reshape(R * T, W)],
                                      axis=0).astype(BF16)
                res = _dot(m3_ref[kah, kb].astype(BF16), rhs)
                groups = []
                for j in range(W // FNET_GDIM):
                    gl = slice(j * FNET_GDIM, (j + 1) * FNET_GDIM)
                    z = jnp.concatenate([res[:R * T, gl], res[R * T:, gl]], axis=1).astype(BF16)
                    groups.append(_dot(z, mg_ref[half * (W // FNET_GDIM) + j]))
                mixed.append(jnp.concatenate(groups, axis=1).reshape(R, T, W))
            uf = jnp.stack(mixed, axis=1).reshape(R, 2 * T, W)
            gate = _silu(zf_ref[:, kb, :, lanes].astype(F32))
            out_ref[:, kb, :, lanes] = (uf * gate).astype(out_ref.dtype)
            return carry

        lax.fori_loop(0, R, stage3, 0, unroll=FNET_UNROLL)


def _fnet(u5, zf5, w_fnet, tables, batch):
    m1, m2, m3, chan = tables
    R, T, W = RADIX, SUBLANES, FNET_HALF
    return pl.pallas_call(
        _fnet_kernel,
        grid=(batch,),
        in_specs=[pl.BlockSpec((None, R, R, R, FNET_WIDTH), lambda b: (b, 0, 0, 0, 0)),
                  pl.BlockSpec((None, R, R, R, FNET_WIDTH), lambda b: (b, 0, 0, 0, 0)),
                  pl.BlockSpec((FNET_GROUPS, FNET_GDIM, FNET_GDIM), lambda b: (0, 0, 0)),
                  pl.BlockSpec(m1.shape, lambda b: (0, 0)),
                  pl.BlockSpec(m2.shape, lambda b: (0, 0, 0)),
                  pl.BlockSpec(m3.shape, lambda b: (0, 0, 0, 0)),
                  pl.BlockSpec(chan.shape, lambda b: (0, 0))],
        out_specs=pl.BlockSpec((None, R, R, R, FNET_WIDTH), lambda b: (b, 0, 0, 0, 0)),
        out_shape=jax.ShapeDtypeStruct((batch, R, R, R, FNET_WIDTH), BF16),
        scratch_shapes=[pltpu.VMEM((2, R, R, T, W), F32), pltpu.VMEM((2, R, R, T, W), F32),
                        pltpu.VMEM((2, R, R, T, W), F32), pltpu.VMEM((2, R, R, T, W), F32),
                        pltpu.VMEM((FNET_GROUPS, 2 * FNET_GDIM, FNET_GDIM), BF16)],
        compiler_params=pltpu.CompilerParams(dimension_semantics=("arbitrary",),
                                             vmem_limit_bytes=VMEM_LIMIT_BYTES),
        name="fnet",
    )(u5, zf5, w_fnet, m1, m2, m3, chan)


def _out_kernel(x_ref, og_ref, zg_ref, yf_ref, w_ref, lng_ref, lnb_ref, o_ref):
    sub = OUT_ROW_TILE // OUT_SUBTILES
    ys = []
    for t in range(OUT_SUBTILES):
        rows = slice(t * sub, (t + 1) * sub)
        y_gla = (og_ref[rows, :].astype(F32) * _silu(zg_ref[rows, :].astype(F32))).astype(BF16)
        ys.append(_dot(jnp.concatenate([y_gla, yf_ref[rows, :]], axis=1), w_ref[...]))
    for t in range(OUT_SUBTILES):
        rows = slice(t * sub, (t + 1) * sub)
        r = DEEPNORM_ALPHA * x_ref[rows, :] + ys[t]
        mu = jnp.mean(r, axis=-1, keepdims=True)
        d = r - mu
        var = jnp.mean(d * d, axis=-1, keepdims=True)
        o_ref[rows, :] = d * lax.rsqrt(var + LN_EPS) * lng_ref[...] + lnb_ref[...]


def _out_proj(x2d, o_gla, zg, y_fnet, w_out, ln_g, ln_b):
    m = x2d.shape[0]
    row = lambda i: (i, 0)
    const = lambda i: (0, 0)
    return pl.pallas_call(
        _out_kernel,
        grid=(m // OUT_ROW_TILE,),
        in_specs=[pl.BlockSpec((OUT_ROW_TILE, D_MODEL), row),
                  pl.BlockSpec((OUT_ROW_TILE, GLA_WIDTH), row),
                  pl.BlockSpec((OUT_ROW_TILE, GLA_WIDTH), row),
                  pl.BlockSpec((OUT_ROW_TILE, FNET_WIDTH), row),
                  pl.BlockSpec((GLA_WIDTH + FNET_WIDTH, D_MODEL), const),
                  pl.BlockSpec((1, D_MODEL), const),
                  pl.BlockSpec((1, D_MODEL), const)],
        out_specs=pl.BlockSpec((OUT_ROW_TILE, D_MODEL), row),
        out_shape=jax.ShapeDtypeStruct((m, D_MODEL), F32),
        compiler_params=pltpu.CompilerParams(dimension_semantics=("arbitrary",),
                                             vmem_limit_bytes=VMEM_LIMIT_BYTES),
        name="out_proj",
    )(x2d, o_gla, zg, y_fnet, w_out, ln_g, ln_b)


def _permute_w_in(w):
    kw, gw, fw, r = GLA_KEY_WIDTH, GLA_WIDTH, FNET_WIDTH, GATE_RANK
    o_q, o_k, o_v = 0, kw, 2 * kw
    o_gf = o_v + gw
    o_zg = o_gf + 2 * r
    o_u = o_zg + gw
    o_zf = o_u + fw
    w = w.astype(BF16)
    gates = jnp.concatenate([w[:, o_gf:o_gf + 2 * r]] * (GATE_PAD // (2 * r)), axis=1)
    q_scale = jnp.asarray(GLA_DK ** -0.5, BF16)
    cols = [w[:, o_q:o_q + kw] * q_scale, w[:, o_k:o_k + kw], w[:, o_v:o_v + gw], w[:, o_zg:o_zg + gw],
            w[:, o_u:o_u + fw], w[:, o_zf:o_zf + fw], gates]
    return jnp.concatenate(cols, axis=1)


def kernel(x, w_in, w_gate_up_fwd, b_gate_fwd, w_gate_up_bwd, b_gate_bwd, gla_norm_g,
           w_fnet, w_out, ln_g, ln_b):
    batch, seq, d = x.shape
    assert (seq, d) == (SEQ, D_MODEL) and w_in.shape[0] == DEPTH
    tables = _fnet_tables()
    for l in range(DEPTH):
        x2d = x.reshape(batch * seq, d)
        q, k, v, zg, u, zf, g = _projection(x2d, _permute_w_in(w_in[l]))

        zero = jnp.zeros((GATE_RANK, GLA_KEY_WIDTH), F32)
        w_bd = jnp.block([[w_gate_up_fwd[l], zero], [zero, w_gate_up_bwd[l]]]) * LOG2_E
        w_hi = w_bd.astype(BF16)
        w_lo = (w_bd - w_hi.astype(F32)).astype(BF16)
        wg = jnp.concatenate([w_hi, w_hi, w_lo, jnp.zeros_like(w_hi)], axis=0)
        bg = jnp.concatenate([b_gate_fwd[l], b_gate_bwd[l]])[None, :] * LOG2_E
        o_gla = _gla(q, k, v, g, wg, bg, gla_norm_g[l][None, :], batch)

        seq_digits = (batch, RADIX, RADIX, RADIX, FNET_WIDTH)
        y_fnet = _fnet(u.reshape(seq_digits), zf.reshape(seq_digits), w_fnet[l], tables, batch)

        out = _out_proj(x2d, o_gla, zg, y_fnet.reshape(batch * seq, FNET_WIDTH),
                        w_out[l].astype(BF16), ln_g[l][None, :], ln_b[l][None, :])
        x = out.reshape(batch, seq, d)
    return x
```

```python
import math

import numpy as np
import jax
import jax.numpy as jnp
from jax import lax
from jax.experimental import pallas as pl
from jax.experimental.pallas import tpu as pltpu

F32 = jnp.float32
BF16 = jnp.bfloat16

D_MODEL = 1024
SEQ = 4096
GLA_HEADS = 4
GLA_DK = 64
GLA_DV = 128
GLA_KEY_WIDTH = GLA_HEADS * GLA_DK
GLA_WIDTH = GLA_HEADS * GLA_DV
GATE_RANK = 16
GATE_TAU = 16.0
FNET_GROUPS = 4
FNET_GDIM = 128
FNET_WIDTH = FNET_GROUPS * FNET_GDIM
LN_EPS = 1e-5
RMS_EPS = 1e-6
DEPTH = 1
DEEPNORM_ALPHA = (2.0 * DEPTH) ** 0.25
LOG2_E = 1.0 / math.log(2.0)

LANES = 128
SUBLANES = 8
VMEM_LIMIT_BYTES = 60 * 1024 * 1024

ROW_TILE = 2048
OUT_ROW_TILE = 1024
OUT_SUBTILES = 4
OUT_X_BUFFERS = 3
GLA_CHUNK = 256
GATE_PAD = LANES
RADIX = 16
FNET_HALF = 256
GLA_SAFE_LOG2_DECAY = 100.0
GLA_PIPE_UNROLL = 15
GLA_UNROLL = 16
FNET_UNROLL = 32

_PROJ_COLS = (("q", GLA_KEY_WIDTH), ("k", GLA_KEY_WIDTH), ("v", GLA_WIDTH), ("zg", GLA_WIDTH),
              ("u", FNET_WIDTH), ("zf", FNET_WIDTH), ("g", GATE_PAD))
PROJ_WIDTH = sum(w for _, w in _PROJ_COLS)


def _dot(a, b):
    return jnp.dot(a, b, preferred_element_type=F32)


def _split2(x):
    hi = x.astype(BF16)
    lo = (x - hi.astype(F32)).astype(BF16)
    return hi, lo


def _dot_f32(a, b):
    ah, al = _split2(a)
    bh, bl = _split2(b)
    return _dot(ah, bh) + _dot(ah, bl) + _dot(al, bh)


def _silu(z):
    h = 0.5 * z
    return h + h * jnp.tanh(h)


def _proj_kernel(x_ref, w_ref, q_ref, k_ref, v_ref, zg_ref, u_ref, zf_ref, g_ref):
    xb = x_ref[...].astype(BF16)
    outs = (q_ref, k_ref, v_ref, zg_ref, u_ref, zf_ref, g_ref)
    off = 0
    for (name, width), o_ref in zip(_PROJ_COLS, outs):
        r = _dot(xb, w_ref[:, off:off + width])
        if name == "g":
            hi, lo = _split2(r)
            lane = lax.broadcasted_iota(jnp.int32, r.shape, 1)
            r = jnp.where((lane >= 2 * GATE_RANK) & (lane < 4 * GATE_RANK), lo, hi)
        o_ref[...] = r.astype(o_ref.dtype)
        off += width


def _projection(x2d, w_perm):
    m = x2d.shape[0]
    grid = (m // ROW_TILE,)
    row = lambda i: (i, 0)
    out_shape = [jax.ShapeDtypeStruct((m, w), BF16) for _, w in _PROJ_COLS]
    out_specs = [pl.BlockSpec((ROW_TILE, w), row) for _, w in _PROJ_COLS]
    return pl.pallas_call(
        _proj_kernel,
        grid=grid,
        in_specs=[pl.BlockSpec((ROW_TILE, D_MODEL), row),
                  pl.BlockSpec((D_MODEL, PROJ_WIDTH), lambda i: (0, 0))],
        out_specs=out_specs,
        out_shape=out_shape,
        compiler_params=pltpu.CompilerParams(dimension_semantics=("arbitrary",),
                                             vmem_limit_bytes=VMEM_LIMIT_BYTES),
        name="in_proj",
    )(x2d, w_perm)


def _gla_kernel(q_ref, k_ref, v_ref, g_ref, wg_ref, bg_ref, gam_ref, y_ref,
                oacc_ref, qd_ref, ki_ref, kdt_ref, upd_ref, dec_ref, dect_ref,
                flag_ref, bsc_ref, qsc_ref, psc_ref):
    C = GLA_CHUNK
    KW = GLA_KEY_WIDTH
    n_chunks = SEQ // C
    row_i = lax.broadcasted_iota(jnp.int32, (C, C), 0)
    col_i = lax.broadcasted_iota(jnp.int32, (C, C), 1)
    lower = row_i >= col_i
    upper = row_i <= col_i
    lane_head = lax.broadcasted_iota(jnp.int32, (C, KW), 1) // GLA_DK
    pair_top = lax.broadcasted_iota(jnp.int32, (2 * GLA_DK, GLA_DV), 0) < GLA_DK
    same_head = (lax.broadcasted_iota(jnp.int32, (2 * GLA_DV, 2 * GLA_DV), 0) // GLA_DV ==
                 lax.broadcasted_iota(jnp.int32, (2 * GLA_DV, 2 * GLA_DV), 1) // GLA_DV)
    pair_mean = jnp.where(same_head, 1.0 / GLA_DV, 0.0).astype(BF16)

    def chunk_rows(n):
        return pl.ds(pl.multiple_of(n * C, C), C)

    def gate_preact(rows):
        return _dot(g_ref[rows, :], wg_ref[...]) + bg_ref[...]

    def gate_logs(z2):
        tau_log2_a = jnp.minimum(z2, 0.0) - jnp.log2(1.0 + jnp.exp2(-jnp.abs(z2)))
        hi = tau_log2_a.astype(BF16)
        mid = (tau_log2_a - hi.astype(F32)).astype(BF16)
        return hi, mid

    def cum_log_decays(hi, mid):
        lo_t = jnp.where(lower, 1.0 / GATE_TAU, 0.0).astype(BF16)
        up_t = jnp.where(upper, 1.0 / GATE_TAU, 0.0).astype(BF16)
        b_f = _dot(jnp.concatenate([lo_t, lo_t], axis=1),
                   jnp.concatenate([hi[:, :KW], mid[:, :KW]], axis=0))
        b_b = _dot(jnp.concatenate([up_t, up_t], axis=1),
                   jnp.concatenate([hi[:, KW:], mid[:, KW:]], axis=0))
        return b_f, b_b

    def scale_qk(n, b_f, b_b):
        rows = chunk_rows(n)
        e_f = b_f[C - 1:C, :]
        e_b = b_b[0:1, :]
        dec_ref[n, 0] = jnp.broadcast_to(jnp.exp2(e_f), (SUBLANES, KW))
        dec_ref[n, 1] = jnp.broadcast_to(jnp.exp2(e_b), (SUBLANES, KW))
        flag_ref[n] = (jnp.max(-jnp.minimum(e_f, e_b)) > GLA_SAFE_LOG2_DECAY).astype(jnp.int32)

        qf = q_ref[rows, :].astype(F32)
        kf = k_ref[rows, :].astype(F32)
        qd_ref[rows, :KW] = (qf * jnp.exp2(b_f)).astype(BF16)
        qd_ref[rows, KW:] = (qf * jnp.exp2(b_b)).astype(BF16)
        ki_ref[rows, :KW] = (kf * jnp.exp2(-b_f)).astype(BF16)
        ki_ref[rows, KW:] = (kf * jnp.exp2(-b_b)).astype(BF16)
        kd = jnp.concatenate([kf * jnp.exp2(e_f - b_f), kf * jnp.exp2(e_b - b_b)], axis=1)
        kdt_ref[n] = jnp.transpose(kd).astype(BF16)

    def scores(n):
        rows = chunk_rows(n)
        nt = (((1,), (1,)), ((), ()))
        qd_f = qd_ref[rows, :KW]
        qd_b = qd_ref[rows, KW:]
        probs = []
        for h in range(GLA_HEADS):
            in_head = lane_head == h
            s_f = lax.dot_general(jnp.where(in_head, qd_f, jnp.zeros_like(qd_f)), ki_ref[rows, :KW],
                                  nt, preferred_element_type=F32)
            s_b = lax.dot_general(jnp.where(in_head, qd_b, jnp.zeros_like(qd_b)), ki_ref[rows, KW:],
                                  nt, preferred_element_type=F32)
            probs.append((jnp.where(lower, s_f, 0.0) + jnp.where(upper, s_b, 0.0)).astype(BF16))
        return probs

    def weigh_values(n, probs):
        rows = chunk_rows(n)
        for h in range(GLA_HEADS):
            vcols = slice(h * GLA_DV, (h + 1) * GLA_DV)
            lhs = jnp.concatenate([probs[h], kdt_ref[n, h * GLA_DK:(h + 1) * GLA_DK, :],
                                   kdt_ref[n, KW + h * GLA_DK:KW + (h + 1) * GLA_DK, :]], axis=0)
            r = _dot(lhs, v_ref[rows, vcols])
            oacc_ref[rows, vcols] = r[:C]
            upd_ref[n, 0, h * GLA_DK:(h + 1) * GLA_DK, :] = r[C:C + GLA_DK]
            upd_ref[n, 1, h * GLA_DK:(h + 1) * GLA_DK, :] = r[C + GLA_DK:]
        for d in range(2):
            dect_ref[n, d] = jnp.transpose(jnp.broadcast_to(dec_ref[n, d][0:1, :], (LANES, KW)))

    def attend_and_scale_next(n, carry):
        z2 = gate_preact(chunk_rows(n + 1))
        probs = scores(n)
        b_next = cum_log_decays(*gate_logs(z2))
        weigh_values(n, probs)
        scale_qk(n + 1, *b_next)
        return carry

    scale_qk(0, *cum_log_decays(*gate_logs(gate_preact(chunk_rows(0)))))
    lax.fori_loop(0, n_chunks - 1, attend_and_scale_next, 0, unroll=GLA_PIPE_UNROLL)
    weigh_values(n_chunks - 1, scores(n_chunks - 1))

    def redo_exact(n, carry):
        @pl.when(flag_ref[n] != 0)
        def _():
            rows = chunk_rows(n)
            nt = (((1,), (1,)), ((), ()))
            b_f, b_b = cum_log_decays(*gate_logs(gate_preact(rows)))
            bsc_ref[:, :KW] = b_f
            bsc_ref[:, KW:] = b_b
            qsc_ref[...] = q_ref[rows, :].astype(F32)
            col = lax.broadcasted_iota(jnp.int32, (SUBLANES, C), 1)
            head_row = (lax.broadcasted_iota(jnp.int32, (SUBLANES, KW), 0) ==
                        lax.broadcasted_iota(jnp.int32, (SUBLANES, KW), 1) // GLA_DK)

            def row_scores(i, c):
                kf = k_ref[rows, :].astype(F32)
                q_heads = jnp.where(head_row, jnp.broadcast_to(qsc_ref[pl.ds(i, 1), :], (SUBLANES, KW)),
                                    0.0).astype(BF16)
                w_f = jnp.exp2(jnp.minimum(bsc_ref[pl.ds(i, 1), :KW] - bsc_ref[:, :KW], 0.0))
                w_b = jnp.exp2(jnp.minimum(bsc_ref[pl.ds(i, 1), KW:] - bsc_ref[:, KW:], 0.0))
                s_f = lax.dot_general(q_heads, (kf * w_f).astype(BF16), nt, preferred_element_type=F32)
                s_b = lax.dot_general(q_heads, (kf * w_b).astype(BF16), nt, preferred_element_type=F32)
                p = jnp.where(col <= i, s_f, 0.0) + jnp.where(col >= i, s_b, 0.0)
                for h in range(GLA_HEADS):
                    psc_ref[h, pl.ds(i, 1), :] = p[h:h + 1, :]
                return c

            lax.fori_loop(0, C, row_scores, 0)
            for h in range(GLA_HEADS):
                vcols = slice(h * GLA_DV, (h + 1) * GLA_DV)
                oacc_ref[rows, vcols] = _dot(psc_ref[h].astype(BF16), v_ref[rows, vcols])
        return carry

    lax.fori_loop(0, n_chunks, redo_exact, 0)

    def enter_state(n, d, state):
        increment = upd_ref[n, d]
        upd_ref[n, d] = state
        return state * dect_ref[n, d] + increment

    def scan_states(j, states):
        s_f, s_b = states
        return enter_state(j, 0, s_f), enter_state(n_chunks - 1 - j, 1, s_b)

    state0 = jnp.zeros((KW, GLA_DV), F32)
    lax.fori_loop(0, n_chunks, scan_states, (state0, state0), unroll=GLA_UNROLL)

    def pair_operand(state_rows):
        sb = state_rows.astype(BF16)
        zero = jnp.zeros_like(sb)
        return jnp.concatenate([jnp.where(pair_top, sb, zero), jnp.where(pair_top, zero, sb)], axis=1)

    def cross_chunk_and_norm(n, carry):
        rows = chunk_rows(n)
        for pair in range(GLA_HEADS // 2):
            kdims = slice(2 * pair * GLA_DK, 2 * (pair + 1) * GLA_DK)
            cols = slice(2 * pair * GLA_DV, 2 * (pair + 1) * GLA_DV)
            q_cat = jnp.concatenate([qd_ref[rows, kdims],
                                     qd_ref[rows, KW + 2 * pair * GLA_DK:KW + 2 * (pair + 1) * GLA_DK]],
                                    axis=1)
            s_cat = jnp.concatenate([pair_operand(upd_ref[n, 0, kdims, :]),
                                     pair_operand(upd_ref[n, 1, kdims, :])], axis=0)
            o_p = oacc_ref[rows, cols] + _dot(q_cat, s_cat)
            ms = _dot((o_p * o_p).astype(BF16), pair_mean)
            y_ref[rows, cols] = (o_p * lax.rsqrt(ms + RMS_EPS) * gam_ref[:, cols]).astype(y_ref.dtype)
        return carry

    lax.fori_loop(0, n_chunks, cross_chunk_and_norm, 0, unroll=GLA_UNROLL)


def _gla(q, k, v, g, wg, bg, gamma, batch):
    tok = lambda b: (b, 0)
    const = lambda b: (0, 0)
    n_chunks = SEQ // GLA_CHUNK
    return pl.pallas_call(
        _gla_kernel,
        grid=(batch,),
        in_specs=[pl.BlockSpec((SEQ, GLA_KEY_WIDTH), tok),
                  pl.BlockSpec((SEQ, GLA_KEY_WIDTH), tok),
                  pl.BlockSpec((SEQ, GLA_WIDTH), tok),
                  pl.BlockSpec((SEQ, GATE_PAD), tok),
                  pl.BlockSpec((GATE_PAD, 2 * GLA_KEY_WIDTH), const),
                  pl.BlockSpec((1, 2 * GLA_KEY_WIDTH), const),
                  pl.BlockSpec((1, GLA_WIDTH), const)],
        out_specs=pl.BlockSpec((SEQ, GLA_WIDTH), tok),
        out_shape=jax.ShapeDtypeStruct((batch * SEQ, GLA_WIDTH), BF16),
        scratch_shapes=[pltpu.VMEM((SEQ, GLA_WIDTH), F32),
                        pltpu.VMEM((SEQ, 2 * GLA_KEY_WIDTH), BF16),
                        pltpu.VMEM((SEQ, 2 * GLA_KEY_WIDTH), BF16),
                        pltpu.VMEM((n_chunks, 2 * GLA_KEY_WIDTH, GLA_CHUNK), BF16),
                        pltpu.VMEM((n_chunks, 2, GLA_KEY_WIDTH, GLA_DV), F32),
                        pltpu.VMEM((n_chunks, 2, SUBLANES, GLA_KEY_WIDTH), F32),
                        pltpu.VMEM((n_chunks, 2, GLA_KEY_WIDTH, LANES), F32),
                        pltpu.SMEM((n_chunks,), jnp.int32),
                        pltpu.VMEM((GLA_CHUNK, 2 * GLA_KEY_WIDTH), F32),
                        pltpu.VMEM((GLA_CHUNK, GLA_KEY_WIDTH), F32),
                        pltpu.VMEM((GLA_HEADS, GLA_CHUNK, GLA_CHUNK), F32)],
        compiler_params=pltpu.CompilerParams(dimension_semantics=("arbitrary",),
                                             vmem_limit_bytes=VMEM_LIMIT_BYTES),
        name="gla",
    )(q, k, v, g, wg, bg, gamma)


def _fnet_tables():
    R = RADIX
    n = SEQ
    a = np.arange(R)
    lo8 = np.arange(SUBLANES)
    eye8 = np.eye(SUBLANES)

    ka = (8 * np.arange(2)[:, None] + lo8[None, :])
    ang = 2 * np.pi * (256 * a[None, None, :] * ka[:, :, None]) / n
    eye16 = np.eye(R)
    def s1(trig):
        t = trig(ang)
        m = t[:, None, :, :, None] * eye16[None, :, None, None, :]
        return m.reshape(2 * R * SUBLANES, R * R)
    m1 = np.concatenate([s1(np.cos), -s1(np.sin)], axis=0)

    def complex_block(theta):
        c = np.cos(theta)
        s = -np.sin(theta)
        def kron(t):
            m = t[:, :, :, None] * eye8[None, :, None, :]
            return m.reshape(t.shape[0] * SUBLANES, t.shape[2] * SUBLANES)
        mr, mi = kron(c), kron(s)
        return np.block([[mr, -mi], [mi, mr]])

    m2 = []
    for kah in range(2):
        k_a = 8 * kah + lo8
        expo = 16 * a[None, None, :] * k_a[None, :, None] + 256 * a[None, None, :] * a[:, None, None]
        m2.append(complex_block(2 * np.pi * expo / n))
    m2 = np.stack(m2)

    m3 = np.zeros((2, R, 2 * R * SUBLANES, 2 * R * SUBLANES))
    for kah in range(2):
        k_a = 8 * kah + lo8
        for kb in range(R):
            expo = (a[None, None, :] * k_a[None, :, None] + 16 * a[None, None, :] * kb
                    + 256 * a[None, None, :] * a[:, None, None])
            m3[kah, kb] = complex_block(2 * np.pi * expo / n)

    c = np.arange(FNET_GDIM)
    ang_c = 2 * np.pi * np.outer(c, c) / FNET_GDIM
    norm = 1.0 / math.sqrt(SEQ * FNET_GDIM)
    chan = np.concatenate([np.cos(ang_c), np.sin(ang_c)], axis=0) * norm
    return jnp.asarray(m1, F32), jnp.asarray(m2, F32), jnp.asarray(m3, F32), jnp.asarray(chan, F32)


def _fnet_kernel(u_ref, zf_ref, wf_ref, m1_ref, m2_ref, m3_ref, chan_ref, out_ref,
                 y1r, y1i, y2r, y2i, mg_ref):
    R = RADIX
    T = SUBLANES
    W = FNET_HALF

    @pl.when(pl.program_id(0) == 0)
    def _():
        for g in range(FNET_GROUPS):
            mg_ref[g] = _dot_f32(chan_ref[...], wf_ref[g]).astype(BF16)

    for half in range(FNET_WIDTH // W):
        lanes = slice(half * W, (half + 1) * W)

        def stage1(sm, carry):
            blk = u_ref[:, sm, :, lanes].reshape(R * R, W)
            res = _dot(m1_ref[...].astype(BF16), blk)
            y1r[:, :, sm, :, :] = res[:R * R].reshape(2, R, T, W)
            y1i[:, :, sm, :, :] = res[R * R:].reshape(2, R, T, W)
            return carry

        lax.fori_loop(0, R, stage1, 0, unroll=FNET_UNROLL)

        def stage2(i, carry):
            kah = i // R
            sl = i % R
            rhs = jnp.concatenate([y1r[kah, sl].reshape(R * T, W), y1i[kah, sl].reshape(R * T, W)],
                                  axis=0).astype(BF16)
            res = _dot(m2_ref[kah].astype(BF16), rhs)
            y2r[kah, :, sl, :, :] = res[:R * T].reshape(R, T, W)
            y2i[kah, :, sl, :, :] = res[R * T:].reshape(R, T, W)
            return carry

        lax.fori_loop(0, 2 * R, stage2, 0, unroll=FNET_UNROLL)

        def stage3(kb, carry):
            mixed = []
            for kah in range(2):
                rhs = jnp.concatenate([y2r[kah, kb].reshape(R * T, W), y2i[kah, kb].reshape(R * T, W)],
                                      axis=0).astype(BF16)
                res = _dot(m3_ref[kah, kb].astype(BF16), rhs)
                groups = []
                for j in range(W // FNET_GDIM):
                    gl = slice(j * FNET_GDIM, (j + 1) * FNET_GDIM)
                    z = jnp.concatenate([res[:R * T, gl], res[R * T:, gl]], axis=1).astype(BF16)
                    groups.append(_dot(z, mg_ref[half * (W // FNET_GDIM) + j]))
                mixed.append(jnp.concatenate(groups, axis=1).reshape(R, T, W))
            uf = jnp.stack(mixed, axis=1).reshape(R, 2 * T, W)
            gate = _silu(zf_ref[:, kb, :, lanes].astype(F32))
            out_ref[:, kb, :, lanes] = (uf * gate).astype(out_ref.dtype)
            return carry

        lax.fori_loop(0, R, stage3, 0, unroll=FNET_UNROLL)


def _fnet(u5, zf5, w_fnet, tables, batch):
    m1, m2, m3, chan = tables
    R, T, W = RADIX, SUBLANES, FNET_HALF
    return pl.pallas_call(
        _fnet_kernel,
        grid=(batch,),
        in_specs=[pl.BlockSpec((None, R, R, R, FNET_WIDTH), lambda b: (b, 0, 0, 0, 0)),
                  pl.BlockSpec((None, R, R, R, FNET_WIDTH), lambda b: (b, 0, 0, 0, 0)),
                  pl.BlockSpec((FNET_GROUPS, FNET_GDIM, FNET_GDIM), lambda b: (0, 0, 0)),
                  pl.BlockSpec(m1.shape, lambda b: (0, 0)),
                  pl.BlockSpec(m2.shape, lambda b: (0, 0, 0)),
                  pl.BlockSpec(m3.shape, lambda b: (0, 0, 0, 0)),
                  pl.BlockSpec(chan.shape, lambda b: (0, 0))],
        out_specs=pl.BlockSpec((None, R, R, R, FNET_WIDTH), lambda b: (b, 0, 0, 0, 0)),
        out_shape=jax.ShapeDtypeStruct((batch, R, R, R, FNET_WIDTH), BF16),
        scratch_shapes=[pltpu.VMEM((2, R, R, T, W), F32), pltpu.VMEM((2, R, R, T, W), F32),
                        pltpu.VMEM((2, R, R, T, W), F32), pltpu.VMEM((2, R, R, T, W), F32),
                        pltpu.VMEM((FNET_GROUPS, 2 * FNET_GDIM, FNET_GDIM), BF16)],
        compiler_params=pltpu.CompilerParams(dimension_semantics=("arbitrary",),
                                             vmem_limit_bytes=VMEM_LIMIT_BYTES),
        name="fnet",
    )(u5, zf5, w_fnet, m1, m2, m3, chan)


def _out_kernel(x_ref, og_ref, zg_ref, yf_ref, w_ref, lng_ref, lnb_ref, o_ref):
    sub = OUT_ROW_TILE // OUT_SUBTILES
    ys = []
    for t in range(OUT_SUBTILES):
        rows = slice(t * sub, (t + 1) * sub)
        y_gla = (og_ref[rows, :].astype(F32) * _silu(zg_ref[rows, :].astype(F32))).astype(BF16)
        ys.append(_dot(jnp.concatenate([y_gla, yf_ref[rows, :]], axis=1), w_ref[...]))
    for t in range(OUT_SUBTILES):
        rows = slice(t * sub, (t + 1) * sub)
        r = DEEPNORM_ALPHA * x_ref[rows, :] + ys[t]
        mu = jnp.mean(r, axis=-1, keepdims=True)
        d = r - mu
        var = jnp.mean(d * d, axis=-1, keepdims=True)
        o_ref[rows, :] = d * lax.rsqrt(var + LN_EPS) * lng_ref[...] + lnb_ref[...]


def _out_ring_kernel(x_hbm, og_ref, zg_ref, yf_ref, w_ref, lng_ref, lnb_ref, o_ref, x_ring, x_sems):
    s = pl.program_id(0)
    n_steps = pl.num_programs(0)

    def x_copy(step):
        slot = step % OUT_X_BUFFERS
        rows = pl.ds(pl.multiple_of(step * OUT_ROW_TILE, OUT_ROW_TILE), OUT_ROW_TILE)
        return pltpu.make_async_copy(x_hbm.at[rows, :], x_ring.at[slot], x_sems.at[slot])

    @pl.when(s == 0)
    def _():
        for step in range(OUT_X_BUFFERS - 1):
            x_copy(step).start()

    ahead = s + (OUT_X_BUFFERS - 1)

    @pl.when(ahead < n_steps)
    def _():
        x_copy(ahead).start()

    x_copy(s).wait()
    _out_kernel(x_ring.at[s % OUT_X_BUFFERS], og_ref, zg_ref, yf_ref, w_ref, lng_ref, lnb_ref, o_ref)


def _out_proj(x2d, o_gla, zg, y_fnet, w_out, ln_g, ln_b):
    m = x2d.shape[0]
    row = lambda i: (i, 0)
    const = lambda i: (0, 0)
    assert m // OUT_ROW_TILE >= OUT_X_BUFFERS
    return pl.pallas_call(
        _out_ring_kernel,
        grid=(m // OUT_ROW_TILE,),
        in_specs=[pl.BlockSpec(memory_space=pl.ANY),
                  pl.BlockSpec((OUT_ROW_TILE, GLA_WIDTH), row),
                  pl.BlockSpec((OUT_ROW_TILE, GLA_WIDTH), row),
                  pl.BlockSpec((OUT_ROW_TILE, FNET_WIDTH), row),
                  pl.BlockSpec((GLA_WIDTH + FNET_WIDTH, D_MODEL), const),
                  pl.BlockSpec((1, D_MODEL), const),
                  pl.BlockSpec((1, D_MODEL), const)],
        out_specs=pl.BlockSpec((OUT_ROW_TILE, D_MODEL), row),
        out_shape=jax.ShapeDtypeStruct((m, D_MODEL), F32),
        scratch_shapes=[pltpu.VMEM((OUT_X_BUFFERS, OUT_ROW_TILE, D_MODEL), F32),
                        pltpu.SemaphoreType.DMA((OUT_X_BUFFERS,))],
        compiler_params=pltpu.CompilerParams(dimension_semantics=("arbitrary",),
                                             vmem_limit_bytes=VMEM_LIMIT_BYTES),
        name="out_proj",
    )(x2d, o_gla, zg, y_fnet, w_out, ln_g, ln_b)


def _permute_w_in(w):
    kw, gw, fw, r = GLA_KEY_WIDTH, GLA_WIDTH, FNET_WIDTH, GATE_RANK
    o_q, o_k, o_v = 0, kw, 2 * kw
    o_gf = o_v + gw
    o_zg = o_gf + 2 * r
    o_u = o_zg + gw
    o_zf = o_u + fw
    gates = jnp.concatenate([w[:, o_gf:o_gf + 2 * r]] * (GATE_PAD // (2 * r)), axis=1)
    cols = [w[:, o_q:o_q + kw] * (GLA_DK ** -0.5), w[:, o_k:o_k + kw], w[:, o_v:o_v + gw], w[:, o_zg:o_zg + gw],
            w[:, o_u:o_u + fw], w[:, o_zf:o_zf + fw], gates]
    return jnp.concatenate(cols, axis=1).astype(BF16)


def kernel(x, w_in, w_gate_up_fwd, b_gate_fwd, w_gate_up_bwd, b_gate_bwd, gla_norm_g,
           w_fnet, w_out, ln_g, ln_b):
    batch, seq, d = x.shape
    assert (seq, d) == (SEQ, D_MODEL) and w_in.shape[0] == DEPTH
    tables = _fnet_tables()
    for l in range(DEPTH):
        x2d = x.reshape(batch * seq, d)
        q, k, v, zg, u, zf, g = _projection(x2d, _permute_w_in(w_in[l]))

        zero = jnp.zeros((GATE_RANK, GLA_KEY_WIDTH), F32)
        w_bd = jnp.block([[w_gate_up_fwd[l], zero], [zero, w_gate_up_bwd[l]]]) * LOG2_E
        w_hi = w_bd.astype(BF16)
        w_lo = (w_bd - w_hi.astype(F32)).astype(BF16)
        wg = jnp.concatenate([w_hi, w_hi, w_lo, jnp.zeros_like(w_hi)], axis=0)
        bg = jnp.concatenate([b_gate_fwd[l], b_gate_bwd[l]])[None, :] * LOG2_E
        o_gla = _gla(q, k, v, g, wg, bg, gla_norm_g[l][None, :], batch)

        seq_digits = (batch, RADIX, RADIX, RADIX, FNET_WIDTH)
        y_fnet = _fnet(u.reshape(seq_digits), zf.reshape(seq_digits), w_fnet[l], tables, batch)

        out = _out_proj(x2d, o_gla, zg, y_fnet.reshape(batch * seq, FNET_WIDTH),
                        w_out[l].astype(BF16), ln_g[l][None, :], ln_b[l][None, :])
        x = out.reshape(batch, seq, d)
    return x
```

```python
import math

import numpy as np
import jax
import jax.numpy as jnp
from jax import lax
from jax.experimental import pallas as pl
from jax.experimental.pallas import tpu as pltpu

F32 = jnp.float32
BF16 = jnp.bfloat16

D_MODEL = 1024
SEQ = 4096
GLA_HEADS = 4
GLA_DK = 64
GLA_DV = 128
GLA_KEY_WIDTH = GLA_HEADS * GLA_DK
GLA_WIDTH = GLA_HEADS * GLA_DV
GATE_RANK = 16
GATE_TAU = 16.0
FNET_GROUPS = 4
FNET_GDIM = 128
FNET_WIDTH = FNET_GROUPS * FNET_GDIM
LN_EPS = 1e-5
RMS_EPS = 1e-6
DEPTH = 1
DEEPNORM_ALPHA = (2.0 * DEPTH) ** 0.25
LOG2_E = 1.0 / math.log(2.0)

LANES = 128
SUBLANES = 8
VMEM_LIMIT_BYTES = 60 * 1024 * 1024

ROW_TILE = 2048
OUT_ROW_TILE = 1024
OUT_SUBTILES = 4
OUT_X_BUFFERS = 3
GLA_CHUNK = 256
GATE_PAD = LANES
RADIX = 16
FNET_HALF = 256
GLA_SAFE_LOG2_DECAY = 100.0
GLA_PIPE_UNROLL = 15
GLA_UNROLL = 16
FNET_UNROLL = 32

_PROJ_COLS = (("q", GLA_KEY_WIDTH), ("k", GLA_KEY_WIDTH), ("v", GLA_WIDTH), ("zg", GLA_WIDTH),
              ("u", FNET_WIDTH), ("zf", FNET_WIDTH), ("g", GATE_PAD))
PROJ_WIDTH = sum(w for _, w in _PROJ_COLS)


def _dot(a, b):
    return jnp.dot(a, b, preferred_element_type=F32)


def _split2(x):
    hi = x.astype(BF16)
    lo = (x - hi.astype(F32)).astype(BF16)
    return hi, lo


def _dot_f32(a, b):
    ah, al = _split2(a)
    bh, bl = _split2(b)
    return _dot(ah, bh) + _dot(ah, bl) + _dot(al, bh)


def _silu(z):
    h = 0.5 * z
    return h + h * jnp.tanh(h)


def _proj_kernel(x_ref, w_ref, q_ref, k_ref, v_ref, zg_ref, u_ref, zf_ref, g_ref):
    xb = x_ref[...].astype(BF16)
    outs = (q_ref, k_ref, v_ref, zg_ref, u_ref, zf_ref, g_ref)
    off = 0
    for (name, width), o_ref in zip(_PROJ_COLS, outs):
        r = _dot(xb, w_ref[:, off:off + width])
        if name == "g":
            hi, lo = _split2(r)
            lane = lax.broadcasted_iota(jnp.int32, r.shape, 1)
            r = jnp.where((lane >= 2 * GATE_RANK) & (lane < 4 * GATE_RANK), lo, hi)
        o_ref[...] = r.astype(o_ref.dtype)
        off += width


def _projection(x2d, w_perm):
    m = x2d.shape[0]
    grid = (m // ROW_TILE,)
    row = lambda i: (i, 0)
    out_shape = [jax.ShapeDtypeStruct((m, w), BF16) for _, w in _PROJ_COLS]
    out_specs = [pl.BlockSpec((ROW_TILE, w), row) for _, w in _PROJ_COLS]
    return pl.pallas_call(
        _proj_kernel,
        grid=grid,
        in_specs=[pl.BlockSpec((ROW_TILE, D_MODEL), row),
                  pl.BlockSpec((D_MODEL, PROJ_WIDTH), lambda i: (0, 0))],
        out_specs=out_specs,
        out_shape=out_shape,
        compiler_params=pltpu.CompilerParams(dimension_semantics=("arbitrary",),
                                             vmem_limit_bytes=VMEM_LIMIT_BYTES),
        name="in_proj",
    )(x2d, w_perm)


def _gla_kernel(q_ref, k_ref, v_ref, g_ref, wg_ref, bg_ref, gam_ref, y_ref,
                oacc_ref, qd_ref, ki_ref, kdt_ref, upd_ref, dec_ref, dect_ref,
                flag_ref, bsc_ref, qsc_ref, psc_ref):
    C = GLA_CHUNK
    KW = GLA_KEY_WIDTH
    n_chunks = SEQ // C
    row_i = lax.broadcasted_iota(jnp.int32, (C, C), 0)
    col_i = lax.broadcasted_iota(jnp.int32, (C, C), 1)
    lower = row_i >= col_i
    upper = row_i <= col_i
    lane_head = lax.broadcasted_iota(jnp.int32, (C, KW), 1) // GLA_DK
    pair_top = lax.broadcasted_iota(jnp.int32, (2 * GLA_DK, GLA_DV), 0) < GLA_DK
    same_head = (lax.broadcasted_iota(jnp.int32, (2 * GLA_DV, 2 * GLA_DV), 0) // GLA_DV ==
                 lax.broadcasted_iota(jnp.int32, (2 * GLA_DV, 2 * GLA_DV), 1) // GLA_DV)
    pair_mean = jnp.where(same_head, 1.0 / GLA_DV, 0.0).astype(BF16)

    def chunk_rows(n):
        return pl.ds(pl.multiple_of(n * C, C), C)

    def gate_preact(rows):
        return _dot(g_ref[rows, :], wg_ref[...]) + bg_ref[...]

    def gate_logs(z2):
        tau_log2_a = jnp.minimum(z2, 0.0) - jnp.log2(1.0 + jnp.exp2(-jnp.abs(z2)))
        hi = tau_log2_a.astype(BF16)
        mid = (tau_log2_a - hi.astype(F32)).astype(BF16)
        return hi, mid

    def cum_log_decays(hi, mid):
        lo_t = jnp.where(lower, 1.0 / GATE_TAU, 0.0).astype(BF16)
        up_t = jnp.where(upper, 1.0 / GATE_TAU, 0.0).astype(BF16)
        b_f = _dot(jnp.concatenate([lo_t, lo_t], axis=1),
                   jnp.concatenate([hi[:, :KW], mid[:, :KW]], axis=0))
        b_b = _dot(jnp.concatenate([up_t, up_t], axis=1),
                   jnp.concatenate([hi[:, KW:], mid[:, KW:]], axis=0))
        return b_f, b_b

    def scale_qk(n, b_f, b_b):
        rows = chunk_rows(n)
        e_f = b_f[C - 1:C, :]
        e_b = b_b[0:1, :]
        dec_ref[n, 0] = jnp.broadcast_to(jnp.exp2(e_f), (SUBLANES, KW))
        dec_ref[n, 1] = jnp.broadcast_to(jnp.exp2(e_b), (SUBLANES, KW))
        flag_ref[n] = (jnp.max(-jnp.minimum(e_f, e_b)) > GLA_SAFE_LOG2_DECAY).astype(jnp.int32)

        qf = q_ref[rows, :].astype(F32)
        kf = k_ref[rows, :].astype(F32)
        qd_ref[rows, :KW] = (qf * jnp.exp2(b_f)).astype(BF16)
        qd_ref[rows, KW:] = (qf * jnp.exp2(b_b)).astype(BF16)
        ki_ref[rows, :KW] = (kf * jnp.exp2(-b_f)).astype(BF16)
        ki_ref[rows, KW:] = (kf * jnp.exp2(-b_b)).astype(BF16)
        kd = jnp.concatenate([kf * jnp.exp2(e_f - b_f), kf * jnp.exp2(e_b - b_b)], axis=1)
        kdt_ref[n] = jnp.transpose(kd).astype(BF16)

    def scores(n):
        rows = chunk_rows(n)
        nt = (((1,), (1,)), ((), ()))
        qd_f = qd_ref[rows, :KW]
        qd_b = qd_ref[rows, KW:]
        probs = []
        for h in range(GLA_HEADS):
            in_head = lane_head == h
            s_f = lax.dot_general(jnp.where(in_head, qd_f, jnp.zeros_like(qd_f)), ki_ref[rows, :KW],
                                  nt, preferred_element_type=F32)
            s_b = lax.dot_general(jnp.where(in_head, qd_b, jnp.zeros_like(qd_b)), ki_ref[rows, KW:],
                                  nt, preferred_element_type=F32)
            probs.append((jnp.where(lower, s_f, 0.0) + jnp.where(upper, s_b, 0.0)).astype(BF16))
        return probs

    def weigh_values(n, probs):
        rows = chunk_rows(n)
        for h in range(GLA_HEADS):
            vcols = slice(h * GLA_DV, (h + 1) * GLA_DV)
            lhs = jnp.concatenate([probs[h], kdt_ref[n, h * GLA_DK:(h + 1) * GLA_DK, :],
                                   kdt_ref[n, KW + h * GLA_DK:KW + (h + 1) * GLA_DK, :]], axis=0)
            r = _dot(lhs, v_ref[rows, vcols])
            oacc_ref[rows, vcols] = r[:C]
            upd_ref[n, 0, h * GLA_DK:(h + 1) * GLA_DK, :] = r[C:C + GLA_DK]
            upd_ref[n, 1, h * GLA_DK:(h + 1) * GLA_DK, :] = r[C + GLA_DK:]
        for d in range(2):
            dect_ref[n, d] = jnp.transpose(jnp.broadcast_to(dec_ref[n, d][0:1, :], (LANES, KW)))

    def attend_and_scale_next(n, carry):
        z2 = gate_preact(chunk_rows(n + 1))
        probs = scores(n)
        b_next = cum_log_decays(*gate_logs(z2))
        weigh_values(n, probs)
        scale_qk(n + 1, *b_next)
        return carry

    scale_qk(0, *cum_log_decays(*gate_logs(gate_preact(chunk_rows(0)))))
    lax.fori_loop(0, n_chunks - 1, attend_and_scale_next, 0, unroll=GLA_PIPE_UNROLL)
    weigh_values(n_chunks - 1, scores(n_chunks - 1))

    def redo_exact(n, carry):
        @pl.when(flag_ref[n] != 0)
        def _():
            rows = chunk_rows(n)
            nt = (((1,), (1,)), ((), ()))
            b_f, b_b = cum_log_decays(*gate_logs(gate_preact(rows)))
            bsc_ref[:, :KW] = b_f
            bsc_ref[:, KW:] = b_b
            qsc_ref[...] = q_ref[rows, :].astype(F32)
            col = lax.broadcasted_iota(jnp.int32, (SUBLANES, C), 1)
            head_row = (lax.broadcasted_iota(jnp.int32, (SUBLANES, KW), 0) ==
                        lax.broadcasted_iota(jnp.int32, (SUBLANES, KW), 1) // GLA_DK)

            def row_scores(i, c):
                kf = k_ref[rows, :].astype(F32)
                q_heads = jnp.where(head_row, jnp.broadcast_to(qsc_ref[pl.ds(i, 1), :], (SUBLANES, KW)),
                                    0.0).astype(BF16)
                w_f = jnp.exp2(jnp.minimum(bsc_ref[pl.ds(i, 1), :KW] - bsc_ref[:, :KW], 0.0))
                w_b = jnp.exp2(jnp.minimum(bsc_ref[pl.ds(i, 1), KW:] - bsc_ref[:, KW:], 0.0))
                s_f = lax.dot_general(q_heads, (kf * w_f).astype(BF16), nt, preferred_element_type=F32)
                s_b = lax.dot_general(q_heads, (kf * w_b).astype(BF16), nt, preferred_element_type=F32)
                p = jnp.where(col <= i, s_f, 0.0) + jnp.where(col >= i, s_b, 0.0)
                for h in range(GLA_HEADS):
                    psc_ref[h, pl.ds(i, 1), :] = p[h:h + 1, :]
                return c

            lax.fori_loop(0, C, row_scores, 0)
            for h in range(GLA_HEADS):
                vcols = slice(h * GLA_DV, (h + 1) * GLA_DV)
                oacc_ref[rows, vcols] = _dot(psc_ref[h].astype(BF16), v_ref[rows, vcols])
        return carry

    lax.fori_loop(0, n_chunks, redo_exact, 0)

    def enter_state(n, d, state):
        increment = upd_ref[n, d]
        upd_ref[n, d] = state
        return state * dect_ref[n, d] + increment

    def scan_states(j, states):
        s_f, s_b = states
        return enter_state(j, 0, s_f), enter_state(n_chunks - 1 - j, 1, s_b)

    state0 = jnp.zeros((KW, GLA_DV), F32)
    lax.fori_loop(0, n_chunks, scan_states, (state0, state0), unroll=GLA_UNROLL)

    def pair_operand(state_rows):
        sb = state_rows.astype(BF16)
        zero = jnp.zeros_like(sb)
        return jnp.concatenate([jnp.where(pair_top, sb, zero), jnp.where(pair_top, zero, sb)], axis=1)

    def cross_chunk_and_norm(n, carry):
        rows = chunk_rows(n)
        for pair in range(GLA_HEADS // 2):
            kdims = slice(2 * pair * GLA_DK, 2 * (pair + 1) * GLA_DK)
            cols = slice(2 * pair * GLA_DV, 2 * (pair + 1) * GLA_DV)
            q_cat = jnp.concatenate([qd_ref[rows, kdims],
                                     qd_ref[rows, KW + 2 * pair * GLA_DK:KW + 2 * (pair + 1) * GLA_DK]],
                                    axis=1)
            s_cat = jnp.concatenate([pair_operand(upd_ref[n, 0, kdims, :]),
                                     pair_operand(upd_ref[n, 1, kdims, :])], axis=0)
            o_p = oacc_ref[rows, cols] + _dot(q_cat, s_cat)
            ms = _dot((o_p * o_p).astype(BF16), pair_mean)
            y_ref[rows, cols] = (o_p * lax.rsqrt(ms + RMS_EPS) * gam_ref[:, cols]).astype(y_ref.dtype)
        return carry

    lax.fori_loop(0, n_chunks, cross_chunk_and_norm, 0, unroll=GLA_UNROLL)


def _gla(q, k, v, g, wg, bg, gamma, batch):
    tok = lambda b: (b, 0)
    const = lambda b: (0, 0)
    n_chunks = SEQ // GLA_CHUNK
    return pl.pallas_call(
        _gla_kernel,
        grid=(batch,),
        in_specs=[pl.BlockSpec((SEQ, GLA_KEY_WIDTH), tok),
                  pl.BlockSpec((SEQ, GLA_KEY_WIDTH), tok),
                  pl.BlockSpec((SEQ, GLA_WIDTH), tok),
                  pl.BlockSpec((SEQ, GATE_PAD), tok),
                  pl.BlockSpec((GATE_PAD, 2 * GLA_KEY_WIDTH), const),
                  pl.BlockSpec((1, 2 * GLA_KEY_WIDTH), const),
                  pl.BlockSpec((1, GLA_WIDTH), const)],
        out_specs=pl.BlockSpec((SEQ, GLA_WIDTH), tok),
        out_shape=jax.ShapeDtypeStruct((batch * SEQ, GLA_WIDTH), BF16),
        scratch_shapes=[pltpu.VMEM((SEQ, GLA_WIDTH), F32),
                        pltpu.VMEM((SEQ, 2 * GLA_KEY_WIDTH), BF16),
                        pltpu.VMEM((SEQ, 2 * GLA_KEY_WIDTH), BF16),
                        pltpu.VMEM((n_chunks, 2 * GLA_KEY_WIDTH, GLA_CHUNK), BF16),
                        pltpu.VMEM((n_chunks, 2, GLA_KEY_WIDTH, GLA_DV), F32),
                        pltpu.VMEM((n_chunks, 2, SUBLANES, GLA_KEY_WIDTH), F32),
                        pltpu.VMEM((n_chunks, 2, GLA_KEY_WIDTH, LANES), F32),
                        pltpu.SMEM((n_chunks,), jnp.int32),
                        pltpu.VMEM((GLA_CHUNK, 2 * GLA_KEY_WIDTH), F32),
                        pltpu.VMEM((GLA_CHUNK, GLA_KEY_WIDTH), F32),
                        pltpu.VMEM((GLA_HEADS, GLA_CHUNK, GLA_CHUNK), F32)],
        compiler_params=pltpu.CompilerParams(dimension_semantics=("arbitrary",),
                                             vmem_limit_bytes=VMEM_LIMIT_BYTES),
        name="gla",
    )(q, k, v, g, wg, bg, gamma)


def _fnet_tables():
    R = RADIX
    n = SEQ
    a = np.arange(R)
    lo8 = np.arange(SUBLANES)
    eye8 = np.eye(SUBLANES)

    ka = (8 * np.arange(2)[:, None] + lo8[None, :])
    ang = 2 * np.pi * (256 * a[None, None, :] * ka[:, :, None]) / n
    eye16 = np.eye(R)
    def s1(trig):
        t = trig(ang)
        m = t[:, None, :, :, None] * eye16[None, :, None, None, :]
        return m.reshape(2 * R * SUBLANES, R * R)
    m1 = np.concatenate([s1(np.cos), -s1(np.sin)], axis=0)

    def complex_block(theta):
        c = np.cos(theta)
        s = -np.sin(theta)
        def kron(t):
            m = t[:, :, :, None] * eye8[None, :, None, :]
            return m.reshape(t.shape[0] * SUBLANES, t.shape[2] * SUBLANES)
        mr, mi = kron(c), kron(s)
        return np.block([[mr, -mi], [mi, mr]])

    m2 = []
    for kah in range(2):
        k_a = 8 * kah + lo8
        expo = 16 * a[None, None, :] * k_a[None, :, None] + 256 * a[None, None, :] * a[:, None, None]
        m2.append(complex_block(2 * np.pi * expo / n))
    m2 = np.stack(m2)

    m3 = np.zeros((2, R, 2 * R * SUBLANES, 2 * R * SUBLANES))
    for kah in range(2):
        k_a = 8 * kah + lo8
        for kb in range(R):
            expo = (a[None, None, :] * k_a[None, :, None] + 16 * a[None, None, :] * kb
                    + 256 * a[None, None, :] * a[:, None, None])
            m3[kah, kb] = complex_block(2 * np.pi * expo / n)

    c = np.arange(FNET_GDIM)
    ang_c = 2 * np.pi * np.outer(c, c) / FNET_GDIM
    norm = 1.0 / math.sqrt(SEQ * FNET_GDIM)
    chan = np.concatenate([np.cos(ang_c), np.sin(ang_c)], axis=0) * norm
    return jnp.asarray(m1, F32), jnp.asarray(m2, F32), jnp.asarray(m3, F32), jnp.asarray(chan, F32)


def _fnet_kernel(u_ref, zf_ref, wf_ref, m1_ref, m2_ref, m3_ref, chan_ref, out_ref,
                 y1r, y1i, y2r, y2i, mg_ref):
    R = RADIX
    T = SUBLANES
    W = FNET_HALF

    @pl.when(pl.program_id(0) == 0)
    def _():
        for g in range(FNET_GROUPS):
            mg_ref[g] = _dot_f32(chan_ref[...], wf_ref[g]).astype(BF16)

    for half in range(FNET_WIDTH // W):
        lanes = slice(half * W, (half + 1) * W)

        def stage1(sm, carry):
            blk = u_ref[:, sm, :, lanes].reshape(R * R, W)
            res = _dot(m1_ref[...].astype(BF16), blk)
            y1r[:, :, sm, :, :] = res[:R * R].reshape(2, R, T, W)
            y1i[:, :, sm, :, :] = res[R * R:].reshape(2, R, T, W)
            return carry

        lax.fori_loop(0, R, stage1, 0, unroll=FNET_UNROLL)

        def stage2(i, carry):
            kah = i // R
            sl = i % R
            rhs = jnp.concatenate([y1r[kah, sl].reshape(R * T, W), y1i[kah, sl].reshape(R * T, W)],
                                  axis=0).astype(BF16)
            res = _dot(m2_ref[kah].astype(BF16), rhs)
            y2r[kah, :, sl, :, :] = res[:R * T].reshape(R, T, W)
            y2i[kah, :, sl, :, :] = res[R * T:].reshape(R, T, W)
            return carry

        lax.fori_loop(0, 2 * R, stage2, 0, unroll=FNET_UNROLL)

        def stage3(kb, carry):
            mixed = []
            for kah in range(2):
                rhs = jnp.concatenate([y2r[kah, kb].reshape(R * T, W), y2i[kah, kb].reshape(R * T, W)],
                                      axis=0).astype(BF16)
                res = _dot(m3_ref[kah, kb].astype(BF16), rhs)
                groups = []
                for j in range(W // FNET_GDIM):
                    gl = slice(j * FNET_GDIM, (j + 1) * FNET_GDIM)
                    z = jnp.concatenate([res[:R * T, gl], res[R * T:, gl]], axis=1).astype(BF16)
                    groups.append(_dot(z, mg_ref[half * (W // FNET_GDIM) + j]))
                mixed.append(jnp.concatenate(groups, axis=1).reshape(R, T, W))
            uf = jnp.stack(mixed, axis=1).reshape(R, 2 * T, W)
            gate = _silu(zf_ref[:, kb, :, lanes].astype(F32))
            out_ref[:, kb, :, lanes] = (uf * gate).astype(out_ref.dtype)
            return carry

        lax.fori_loop(0, R, stage3, 0, unroll=FNET_UNROLL)


def _fnet(u5, zf5, w_fnet, tables, batch):
    m1, m2, m3, chan = tables
    R, T, W = RADIX, SUBLANES, FNET_HALF
    return pl.pallas_call(
        _fnet_kernel,
        grid=(batch,),
        in_specs=[pl.BlockSpec((None, R, R, R, FNET_WIDTH), lambda b: (b, 0, 0, 0, 0)),
                  pl.BlockSpec((None, R, R, R, FNET_WIDTH), lambda b: (b, 0, 0, 0, 0)),
                  pl.BlockSpec((FNET_GROUPS, FNET_GDIM, FNET_GDIM), lambda b: (0, 0, 0)),
                  pl.BlockSpec(m1.shape, lambda b: (0, 0)),
                  pl.BlockSpec(m2.shape, lambda b: (0, 0, 0)),
                  pl.BlockSpec(m3.shape, lambda b: (0, 0, 0, 0)),
                  pl.BlockSpec(chan.shape, lambda b: (0, 0))],
        out_specs=pl.BlockSpec((None, R, R, R, FNET_WIDTH), lambda b: (b, 0, 0, 0, 0)),
        out_shape=jax.ShapeDtypeStruct((batch, R, R, R, FNET_WIDTH), BF16),
        scratch_shapes=[pltpu.VMEM((2, R, R, T, W), F32), pltpu.VMEM((2, R, R, T, W), F32),
                        pltpu.VMEM((2, R, R, T, W), F32), pltpu.VMEM((2, R, R, T, W), F32),
                        pltpu.VMEM((FNET_GROUPS, 2 * FNET_GDIM, FNET_GDIM), BF16)],
        compiler_params=pltpu.CompilerParams(dimension_semantics=("arbitrary",),
                                             vmem_limit_bytes=VMEM_LIMIT_BYTES),
        name="fnet",
    )(u5, zf5, w_fnet, m1, m2, m3, chan)


def _out_kernel(x_ref, og_ref, zg_ref, yf_ref, w_ref, lng_ref, lnb_ref, o_ref):
    sub = OUT_ROW_TILE // OUT_SUBTILES
    ys = []
    for t in range(OUT_SUBTILES):
        rows = slice(t * sub, (t + 1) * sub)
        y_gla = (og_ref[rows, :].astype(F32) * _silu(zg_ref[rows, :].astype(F32))).astype(BF16)
        ys.append(_dot(jnp.concatenate([y_gla, yf_ref[rows, :]], axis=1), w_ref[...]))
    for t in range(OUT_SUBTILES):
        rows = slice(t * sub, (t + 1) * sub)
        r = DEEPNORM_ALPHA * x_ref[rows, :] + ys[t]
        mu = jnp.mean(r, axis=-1, keepdims=True)
        d = r - mu
        var = jnp.mean(d * d, axis=-1, keepdims=True)
        o_ref[rows, :] = d * lax.rsqrt(var + LN_EPS) * lng_ref[...] + lnb_ref[...]


def _out_ring_kernel(x_hbm, og_hbm, zg_hbm, yf_hbm, w_ref, lng_ref, lnb_ref, o_ref,
                     x_ring, og_ring, zg_ring, yf_ring, sems):
    s = pl.program_id(0)
    n_steps = pl.num_programs(0)
    streams = ((x_hbm, x_ring), (og_hbm, og_ring), (zg_hbm, zg_ring), (yf_hbm, yf_ring))

    def copies(step):
        slot = step % OUT_X_BUFFERS
        rows = pl.ds(pl.multiple_of(step * OUT_ROW_TILE, OUT_ROW_TILE), OUT_ROW_TILE)
        return [pltpu.make_async_copy(hbm.at[rows, :], ring.at[slot], sems.at[i, slot])
                for i, (hbm, ring) in enumerate(streams)]

    @pl.when(s == 0)
    def _():
        for step in range(OUT_X_BUFFERS - 1):
            for c in copies(step):
                c.start()

    ahead = s + (OUT_X_BUFFERS - 1)

    @pl.when(ahead < n_steps)
    def _():
        for c in copies(ahead):
            c.start()

    for c in copies(s):
        c.wait()
    slot = s % OUT_X_BUFFERS
    _out_kernel(x_ring.at[slot], og_ring.at[slot], zg_ring.at[slot], yf_ring.at[slot],
                w_ref, lng_ref, lnb_ref, o_ref)


def _out_proj(x2d, o_gla, zg, y_fnet, w_out, ln_g, ln_b):
    m = x2d.shape[0]
    row = lambda i: (i, 0)
    const = lambda i: (0, 0)
    assert m // OUT_ROW_TILE >= OUT_X_BUFFERS
    return pl.pallas_call(
        _out_ring_kernel,
        grid=(m // OUT_ROW_TILE,),
        in_specs=[pl.BlockSpec(memory_space=pl.ANY),
                  pl.BlockSpec(memory_space=pl.ANY),
                  pl.BlockSpec(memory_space=pl.ANY),
                  pl.BlockSpec(memory_space=pl.ANY),
                  pl.BlockSpec((GLA_WIDTH + FNET_WIDTH, D_MODEL), const),
                  pl.BlockSpec((1, D_MODEL), const),
                  pl.BlockSpec((1, D_MODEL), const)],
        out_specs=pl.BlockSpec((OUT_ROW_TILE, D_MODEL), row),
        out_shape=jax.ShapeDtypeStruct((m, D_MODEL), F32),
        scratch_shapes=[pltpu.VMEM((OUT_X_BUFFERS, OUT_ROW_TILE, D_MODEL), F32),
                        pltpu.VMEM((OUT_X_BUFFERS, OUT_ROW_TILE, GLA_WIDTH), BF16),
                        pltpu.VMEM((OUT_X_BUFFERS, OUT_ROW_TILE, GLA_WIDTH), BF16),
                        pltpu.VMEM((OUT_X_BUFFERS, OUT_ROW_TILE, FNET_WIDTH), BF16),
                        pltpu.SemaphoreType.DMA((4, OUT_X_BUFFERS))],
        compiler_params=pltpu.CompilerParams(dimension_semantics=("arbitrary",),
                                             vmem_limit_bytes=VMEM_LIMIT_BYTES),
        name="out_proj",
    )(x2d, o_gla, zg, y_fnet, w_out, ln_g, ln_b)


def _permute_w_in(w):
    kw, gw, fw, r = GLA_KEY_WIDTH, GLA_WIDTH, FNET_WIDTH, GATE_RANK
    o_q, o_k, o_v = 0, kw, 2 * kw
    o_gf = o_v + gw
    o_zg = o_gf + 2 * r
    o_u = o_zg + gw
    o_zf = o_u + fw
    gates = jnp.concatenate([w[:, o_gf:o_gf + 2 * r]] * (GATE_PAD // (2 * r)), axis=1)
    cols = [w[:, o_q:o_q + kw] * (GLA_DK ** -0.5), w[:, o_k:o_k + kw], w[:, o_v:o_v + gw], w[:, o_zg:o_zg + gw],
            w[:, o_u:o_u + fw], w[:, o_zf:o_zf + fw], gates]
    return jnp.concatenate(cols, axis=1).astype(BF16)


def kernel(x, w_in, w_gate_up_fwd, b_gate_fwd, w_gate_up_bwd, b_gate_bwd, gla_norm_g,
           w_fnet, w_out, ln_g, ln_b):
    batch, seq, d = x.shape
    assert (seq, d) == (SEQ, D_MODEL) and w_in.shape[0] == DEPTH
    tables = _fnet_tables()
    for l in range(DEPTH):
        x2d = x.reshape(batch * seq, d)
        q, k, v, zg, u, zf, g = _projection(x2d, _permute_w_in(w_in[l]))

        zero = jnp.zeros((GATE_RANK, GLA_KEY_WIDTH), F32)
        w_bd = jnp.block([[w_gate_up_fwd[l], zero], [zero, w_gate_up_bwd[l]]]) * LOG2_E
        w_hi = w_bd.astype(BF16)
        w_lo = (w_bd - w_hi.astype(F32)).astype(BF16)
        wg = jnp.concatenate([w_hi, w_hi, w_lo, jnp.zeros_like(w_hi)], axis=0)
        bg = jnp.concatenate([b_gate_fwd[l], b_gate_bwd[l]])[None, :] * LOG2_E
        o_gla = _gla(q, k, v, g, wg, bg, gla_norm_g[l][None, :], batch)

        seq_digits = (batch, RADIX, RADIX, RADIX, FNET_WIDTH)
        y_fnet = _fnet(u.reshape(seq_digits), zf.reshape(seq_digits), w_fnet[l], tables, batch)

        out = _out_proj(x2d, o_gla, zg, y_fnet.reshape(batch * seq, FNET_WIDTH),
                        w_out[l].astype(BF16), ln_g[l][None, :], ln_b[l][None, :])
        x = out.reshape(batch, seq, d)
    return x
```
